```python
import math
import jax, jax.numpy as jnp
from jax import lax
import numpy as np

D_MODEL = 1024
BATCH = 4
SEQ = 4096
DEPTH = 2
DEC_BATCH = 128
DEC_SEQ = 4
PAST_LEN = 2048
PAGE_SIZE = 128

N_A_LAYERS = DEPTH // 2
N_B_LAYERS = DEPTH - N_A_LAYERS
D_FF = 2816
D_RNN = D_MODEL
N_RG_BLOCKS = 8
RG_BLOCK = D_RNN // N_RG_BLOCKS
CONV_W = 4
RG_C = 8.0
N_HEADS = 8
HEAD_DIM = D_MODEL // (2 * N_HEADS)
V_DIM = 2 * HEAD_DIM
Q_DIM = N_HEADS * 2 * HEAD_DIM
K_DIM = N_HEADS * 2 * HEAD_DIM
KV_DIM = K_DIM + N_HEADS * V_DIM
PLE_DIM = 256
ROPE_THETA = 10000.0
Q_BLOCK = 128
EPS = 1e-6

kernel_name = "hawk_yoco_diff_attention_step"


def rmsnorm(x, g):
    xf = x.astype(jnp.float32)
    y = xf * lax.rsqrt(jnp.mean(xf * xf, axis=-1, keepdims=True) + EPS) * g.astype(jnp.float32)
    return y.astype(x.dtype)


def swiglu(x, w_gu, w_down):
    g, u = jnp.split(x @ w_gu, 2, axis=-1)
    return (jax.nn.silu(g) * u) @ w_down


def rope(x, pos):
    half = HEAD_DIM // 2
    inv = jnp.power(ROPE_THETA, -jnp.arange(half, dtype=jnp.float32) * 2.0 / HEAD_DIM)
    ang = pos.astype(jnp.float32)[:, None] * inv[None, :]
    cos = jnp.cos(ang)[None, :, None, None, :]
    sin = jnp.sin(ang)[None, :, None, None, :]
    xf = x.astype(jnp.float32)
    x1, x2 = xf[..., :half], xf[..., half:]
    return jnp.concatenate([x1 * cos - x2 * sin, x2 * cos + x1 * sin], axis=-1).astype(x.dtype)


def _lin_combine(e1, e2):
    a1, b1 = e1
    a2, b2 = e2
    return a1 * a2, a2 * b1 + b2


def recurrent_block(xn, conv_prev, h0, w_in, conv_w, conv_b, w_a, b_a, w_i, b_i, lam, w_out):
    B, S = xn.shape[:2]
    gate, u = jnp.split(xn @ w_in, 2, axis=-1)
    u_ext = jnp.concatenate([conv_prev.astype(u.dtype), u], axis=1)
    conv = conv_b + u_ext[:, 0:S] * conv_w[0]
    for k in range(1, CONV_W):
        conv = conv + u_ext[:, k:k + S] * conv_w[k]
    new_conv = u_ext[:, -(CONV_W - 1):]
    cb = conv.reshape(B, S, N_RG_BLOCKS, RG_BLOCK)
    r = jax.nn.sigmoid(jnp.einsum('bsnd,nde->bsne', cb, w_a).reshape(B, S, D_RNN) + b_a)
    i = jax.nn.sigmoid(jnp.einsum('bsnd,nde->bsne', cb, w_i).reshape(B, S, D_RNN) + b_i)
    log_a = -RG_C * r.astype(jnp.float32) * jax.nn.softplus(-lam.astype(jnp.float32))
    a = jnp.exp(log_a)
    mult = jnp.sqrt(-jnp.expm1(2.0 * log_a))
    b = mult * (i * conv).astype(jnp.float32)
    b = b.at[:, 0].add(a[:, 0] * h0.astype(jnp.float32))
    _, h = lax.associative_scan(_lin_combine, (a, b), axis=1)
    new_h = h[:, -1].astype(xn.dtype)
    y = (jax.nn.gelu(gate) * h.astype(gate.dtype)) @ w_out
    return y, new_conv, new_h


def shared_kv(x, pos, kv_norm, w_kv):
    B, S = x.shape[:2]
    kv = rmsnorm(x, kv_norm) @ w_kv
    k, v = jnp.split(kv, [K_DIM], axis=-1)
    k = rope(k.reshape(B, S, N_HEADS, 2, HEAD_DIM), pos).reshape(B, S, 2 * N_HEADS, HEAD_DIM)
    v = v.reshape(B, S, N_HEADS, V_DIM)
    return k, v


def prompt_diff_attention(q, k, v):
    B, S = q.shape[:2]
    nblk = S // Q_BLOCK
    scale = HEAD_DIM ** -0.5
    qb = jnp.moveaxis(q.reshape(B, nblk, Q_BLOCK, N_HEADS, 2, HEAD_DIM), 1, 0)
    k5 = k.reshape(B, S, N_HEADS, 2, HEAD_DIM)
    kpos = jnp.arange(S)

    def one_block(args):
        qblk, j = args
        qpos = j * Q_BLOCK + jnp.arange(Q_BLOCK)
        s = jnp.einsum('bqhcd,bkhcd->bhcqk', qblk, k5).astype(jnp.float32) * scale
        s = jnp.where(kpos[None, :] <= qpos[:, None], s, -jnp.inf)
        p = jax.nn.softmax(s, axis=-1).astype(v.dtype)
        return jnp.einsum('bhcqk,bkhv->bqhcv', p, v)

    o = lax.map(one_block, (qb, jnp.arange(nblk)))
    return jnp.moveaxis(o, 0, 1).reshape(B, S, N_HEADS, 2, V_DIM)


def sample_diff_attention(q, k_new, v_new, cache_k, cache_v, page_table):
    Bd, Sq = q.shape[:2]
    scale = HEAD_DIM ** -0.5
    k5 = k_new.reshape(Bd, Sq, N_HEADS, 2, HEAD_DIM)
    s = jnp.einsum('bqhcd,bkhcd->bhcqk', q, k5).astype(jnp.float32) * scale
    causal = jnp.arange(Sq)[None, :] <= jnp.arange(Sq)[:, None]
    s = jnp.where(causal, s, -jnp.inf)
    m = jnp.max(s, axis=-1)
    p = jnp.exp(s - m[..., None])
    l = jnp.sum(p, axis=-1)
    acc = jnp.einsum('bhcqk,bkhv->bhcqv', p, v_new.astype(jnp.float32))

    def page_step(carry, page_ids):
        m, l, acc = carry
        kp = cache_k[page_ids].reshape(Bd, PAGE_SIZE, N_HEADS, 2, HEAD_DIM)
        vp = cache_v[page_ids]
        sp = jnp.einsum('bqhcd,bkhcd->bhcqk', q, kp.astype(q.dtype)).astype(jnp.float32) * scale
        m_new = jnp.maximum(m, jnp.max(sp, axis=-1))
        corr = jnp.exp(m - m_new)
        pp = jnp.exp(sp - m_new[..., None])
        l = l * corr + jnp.sum(pp, axis=-1)
        acc = acc * corr[..., None] + jnp.einsum('bhcqk,bkhv->bhcqv', pp, vp.astype(jnp.float32))
        return (m_new, l, acc), None

    (m, l, acc), _ = lax.scan(page_step, (m, l, acc), page_table.T)
    o = acc / l[..., None]
    return jnp.moveaxis(o, 3, 1).astype(q.dtype)


def diff_output(o, lam_q1, lam_k1, lam_q2, lam_k2, subln, w_o, lambda_init):
    B, S = o.shape[:2]
    lam = (jnp.exp(jnp.sum(lam_q1.astype(jnp.float32) * lam_k1.astype(jnp.float32)))
           - jnp.exp(jnp.sum(lam_q2.astype(jnp.float32) * lam_k2.astype(jnp.float32)))
           + lambda_init)
    d = o[..., 0, :].astype(jnp.float32) - lam * o[..., 1, :].astype(jnp.float32)
    d = rmsnorm(d, subln) * (1.0 - lambda_init)
    return d.reshape(B, S, N_HEADS * V_DIM).astype(o.dtype) @ w_o


def layer_stack(x, p, pos, conv_prev, h_prev, attend, W):
    B, S = x.shape[:2]
    new_conv, new_h = [], []
    k = v = None
    for i in range(DEPTH):
        x = x + 0.5 * swiglu(rmsnorm(x, W['ffn1_norm'][i]), W['ffn1_w_gu'][i], W['ffn1_w_down'][i])
        xn = rmsnorm(x, W['mix_norm'][i])
        if i < N_A_LAYERS:
            y, c, h = recurrent_block(xn, conv_prev[i], h_prev[i], W['rg_w_in'][i], W['rg_conv_w'][i],
                                      W['rg_conv_b'][i], W['rg_w_a'][i], W['rg_b_a'][i], W['rg_w_i'][i],
                                      W['rg_b_i'][i], W['rg_lambda'][i], W['rg_w_out'][i])
            new_conv.append(c)
            new_h.append(h)
        else:
            j = i - N_A_LAYERS
            q = rope((xn @ W['attn_w_q'][j]).reshape(B, S, N_HEADS, 2, HEAD_DIM), pos)
            o = attend(q, k, v)
            lambda_init = 0.8 - 0.6 * math.exp(-0.3 * i)
            y = diff_output(o, W['lambda_q1'][j], W['lambda_k1'][j], W['lambda_q2'][j], W['lambda_k2'][j],
                            W['attn_subln'][j], W['attn_w_o'][j], lambda_init)
        x = x + y
        x = x + 0.5 * swiglu(rmsnorm(x, W['ffn2_norm'][i]), W['ffn2_w_gu'][i], W['ffn2_w_down'][i])
        gate = jax.nn.sigmoid(rmsnorm(x, W['ple_norm'][i]) @ W['ple_w_gate'][i])
        x = x + gate * (p[i] @ W['ple_w_proj'][i])
        if i == N_A_LAYERS - 1:
            k, v = shared_kv(x, pos, W['kv_norm'], W['w_kv'])
    y = rmsnorm(x, W['final_norm'])
    return y, k, v, jnp.stack(new_conv), jnp.stack(new_h)


def setup_inputs(seed: int = 0) -> dict:
    key = jax.random.key(seed)
    ks = iter(jax.random.split(key, 48))
    f32 = jnp.float32

    def nrm(shape, scale):
        return jax.random.normal(next(ks), shape, f32) * scale

    def gain(shape):
        return 1.0 + nrm(shape, 0.01)

    n_pages = PAST_LEN // PAGE_SIZE
    n_used = DEC_BATCH * n_pages
    n_pool = n_used + max(1, n_used // 4)
    page_table = jax.random.permutation(next(ks), n_pool)[:n_used].reshape(DEC_BATCH, n_pages).astype(jnp.int32)

    u = jax.random.uniform(next(ks), (N_A_LAYERS, D_RNN), f32, 0.9, 0.999)
    a0 = jnp.power(u, 1.0 / RG_C)
    rg_lambda = jnp.log(a0) - jnp.log1p(-a0)

    return {
        'x_prompt': nrm((BATCH, SEQ, D_MODEL), 1.0),
        'x_sample': nrm((DEC_BATCH, DEC_SEQ, D_MODEL), 1.0),
        'p_prompt': nrm((DEPTH, BATCH, SEQ, PLE_DIM), 1.0),
        'p_sample': nrm((DEPTH, DEC_BATCH, DEC_SEQ, PLE_DIM), 1.0),
        'cache_k': nrm((n_pool, PAGE_SIZE, 2 * N_HEADS, HEAD_DIM), 1.0),
        'cache_v': nrm((n_pool, PAGE_SIZE, N_HEADS, V_DIM), 1.0),
        'page_table': page_table,
        'state_conv': nrm((N_A_LAYERS, DEC_BATCH, CONV_W - 1, D_RNN), 1.0),
        'state_rglru': nrm((N_A_LAYERS, DEC_BATCH, D_RNN), 0.5),
        'ffn1_norm': gain((DEPTH, D_MODEL)),
        'ffn1_w_gu': nrm((DEPTH, D_MODEL, 2 * D_FF), D_MODEL ** -0.5),
        'ffn1_w_down': nrm((DEPTH, D_FF, D_MODEL), D_FF ** -0.5),
        'mix_norm': gain((DEPTH, D_MODEL)),
        'rg_w_in': nrm((N_A_LAYERS, D_MODEL, 2 * D_RNN), D_MODEL ** -0.5),
        'rg_conv_w': nrm((N_A_LAYERS, CONV_W, D_RNN), CONV_W ** -0.5),
        'rg_conv_b': nrm((N_A_LAYERS, D_RNN), 0.01),
        'rg_w_a': nrm((N_A_LAYERS, N_RG_BLOCKS, RG_BLOCK, RG_BLOCK), RG_BLOCK ** -0.5),
        'rg_b_a': nrm((N_A_LAYERS, D_RNN), 0.01),
        'rg_w_i': nrm((N_A_LAYERS, N_RG_BLOCKS, RG_BLOCK, RG_BLOCK), RG_BLOCK ** -0.5),
        'rg_b_i': nrm((N_A_LAYERS, D_RNN), 0.01),
        'rg_lambda': rg_lambda,
        'rg_w_out': nrm((N_A_LAYERS, D_RNN, D_MODEL), D_RNN ** -0.5),
        'kv_norm': gain((D_MODEL,)),
        'w_kv': nrm((D_MODEL, KV_DIM), D_MODEL ** -0.5),
        'attn_w_q': nrm((N_B_LAYERS, D_MODEL, Q_DIM), D_MODEL ** -0.5),
        'lambda_q1': nrm((N_B_LAYERS, HEAD_DIM), 0.1),
        'lambda_k1': nrm((N_B_LAYERS, HEAD_DIM), 0.1),
        'lambda_q2': nrm((N_B_LAYERS, HEAD_DIM), 0.1),
        'lambda_k2': nrm((N_B_LAYERS, HEAD_DIM), 0.1),
        'attn_subln': gain((N_B_LAYERS, V_DIM)),
        'attn_w_o': nrm((N_B_LAYERS, N_HEADS * V_DIM, D_MODEL), (N_HEADS * V_DIM) ** -0.5),
        'ffn2_norm': gain((DEPTH, D_MODEL)),
        'ffn2_w_gu': nrm((DEPTH, D_MODEL, 2 * D_FF), D_MODEL ** -0.5),
        'ffn2_w_down': nrm((DEPTH, D_FF, D_MODEL), D_FF ** -0.5),
        'ple_norm': gain((DEPTH, D_MODEL)),
        'ple_w_gate': nrm((DEPTH, D_MODEL, D_MODEL), D_MODEL ** -0.5),
        'ple_w_proj': nrm((DEPTH, PLE_DIM, D_MODEL), PLE_DIM ** -0.5),
        'final_norm': gain((D_MODEL,)),
    }


def reference(x_prompt, x_sample, p_prompt, p_sample, cache_k, cache_v, page_table, state_conv, state_rglru,
              ffn1_norm, ffn1_w_gu, ffn1_w_down, mix_norm, rg_w_in, rg_conv_w, rg_conv_b, rg_w_a, rg_b_a,
              rg_w_i, rg_b_i, rg_lambda, rg_w_out, kv_norm, w_kv, attn_w_q, lambda_q1, lambda_k1, lambda_q2,
              lambda_k2, attn_subln, attn_w_o, ffn2_norm, ffn2_w_gu, ffn2_w_down, ple_norm, ple_w_gate,
              ple_w_proj, final_norm):
    W = {
        'ffn1_norm': ffn1_norm, 'ffn1_w_gu': ffn1_w_gu, 'ffn1_w_down': ffn1_w_down, 'mix_norm': mix_norm,
        'rg_w_in': rg_w_in, 'rg_conv_w': rg_conv_w, 'rg_conv_b': rg_conv_b, 'rg_w_a': rg_w_a,
        'rg_b_a': rg_b_a, 'rg_w_i': rg_w_i, 'rg_b_i': rg_b_i, 'rg_lambda': rg_lambda, 'rg_w_out': rg_w_out,
        'kv_norm': kv_norm, 'w_kv': w_kv, 'attn_w_q': attn_w_q, 'lambda_q1': lambda_q1,
        'lambda_k1': lambda_k1, 'lambda_q2': lambda_q2, 'lambda_k2': lambda_k2, 'attn_subln': attn_subln,
        'attn_w_o': attn_w_o, 'ffn2_norm': ffn2_norm, 'ffn2_w_gu': ffn2_w_gu, 'ffn2_w_down': ffn2_w_down,
        'ple_norm': ple_norm, 'ple_w_gate': ple_w_gate, 'ple_w_proj': ple_w_proj, 'final_norm': final_norm,
    }
    B, S = x_prompt.shape[:2]
    pos_prompt = jnp.arange(S, dtype=jnp.float32)
    conv0 = jnp.zeros((N_A_LAYERS, B, CONV_W - 1, D_RNN), x_prompt.dtype)
    h0 = jnp.zeros((N_A_LAYERS, B, D_RNN), x_prompt.dtype)
    y_prompt, new_k_prompt, new_v_prompt, new_conv_prompt, new_h_prompt = layer_stack(
        x_prompt, p_prompt, pos_prompt, conv0, h0, prompt_diff_attention, W)

    Sd = x_sample.shape[1]
    pos_sample = (PAST_LEN + jnp.arange(Sd)).astype(jnp.float32)

    def attend_sample(q, k, v):
        return sample_diff_attention(q, k, v, cache_k, cache_v, page_table)

    y_sample, new_k_sample, new_v_sample, new_conv_sample, new_h_sample = layer_stack(
        x_sample, p_sample, pos_sample, state_conv, state_rglru, attend_sample, W)

    return (y_prompt, y_sample, new_k_prompt, new_v_prompt, new_k_sample, new_v_sample,
            new_conv_prompt, new_h_prompt, new_conv_sample, new_h_sample)
```

```python
import functools
import math

import jax
import jax.numpy as jnp
from jax import lax
from jax.experimental import pallas as pl
from jax.experimental.pallas import tpu as pltpu

F32 = jnp.float32
BF16 = jnp.bfloat16

EPS = 1e-6
N_HEADS = 8
HEAD_DIM = 64
V_DIM = 2 * HEAD_DIM
N_RG_BLOCKS = 8
CONV_W = 4
RG_C = 8.0
ROPE_THETA = 10000.0
LANES = 128
SUBLANES = 8
NEG_BIG = -1e30
VMEM_LIMIT = 56 * 1024 * 1024

TOKEN_TILE = 512
SEQ_TILE = 512
ATTN_TILE = 512
PAGES_PER_STEP = 4


def _cparams(sem):
    return pltpu.CompilerParams(dimension_semantics=sem, vmem_limit_bytes=VMEM_LIMIT)


def _rms(x, g):
    return x * lax.rsqrt(jnp.mean(x * x, axis=-1, keepdims=True) + EPS) * g


def _rope(x, cos, sin_hi, sin_lo):
    outs = []
    for j in range(x.shape[1] // LANES):
        xj = x[:, j * LANES:(j + 1) * LANES]
        outs.append(xj * cos
                    + pltpu.roll(xj, LANES - HEAD_DIM // 2, 1) * sin_lo
                    + pltpu.roll(xj, HEAD_DIM // 2, 1) * sin_hi)
    return jnp.concatenate(outs, axis=1)


def _gelu_tanh(x):
    return 0.5 * x * (1.0 + jnp.tanh(math.sqrt(2.0 / math.pi) * (x + 0.044715 * (x * x * x))))


def _softplus(z):
    return jnp.maximum(z, 0.0) + jnp.log1p(jnp.exp(-jnp.abs(z)))


def _ffn_body(x_ref, g_ref, wg_ref, wu_ref, wd_ref, o_ref, n_ref, acc_ref):
    f = pl.program_id(1)

    @pl.when(f == 0)
    def _():
        n_ref[...] = _rms(x_ref[...], g_ref[...]).astype(BF16)
        acc_ref[...] = jnp.zeros_like(acc_ref)

    n = n_ref[...]
    g = jnp.dot(n, wg_ref[...], preferred_element_type=F32)
    u = jnp.dot(n, wu_ref[...], preferred_element_type=F32)
    h = (g * jax.nn.sigmoid(g) * u).astype(BF16)
    acc_ref[...] += jnp.dot(h, wd_ref[...], preferred_element_type=F32)

    @pl.when(f == pl.num_programs(1) - 1)
    def _():
        o_ref[...] = x_ref[...] + 0.5 * acc_ref[...]


def _ffn_split(d_ff):
    for nf in (2, 4, 1):
        if d_ff % (nf * LANES) == 0:
            return nf
    raise ValueError(f"unsupported FFN width {d_ff}")


def _ffn(x, g, w_gu, w_down):
    T, D = x.shape
    d_ff = w_down.shape[0]
    nf = _ffn_split(d_ff)
    tf = d_ff // nf
    tm = min(TOKEN_TILE, T)
    return pl.pallas_call(
        _ffn_body,
        out_shape=jax.ShapeDtypeStruct((T, D), F32),
        grid=(T // tm, nf),
        in_specs=[
            pl.BlockSpec((tm, D), lambda i, f: (i, 0)),
            pl.BlockSpec((1, D), lambda i, f: (0, 0)),
            pl.BlockSpec((D, tf), lambda i, f: (0, f)),
            pl.BlockSpec((D, tf), lambda i, f: (0, f + nf)),
            pl.BlockSpec((tf, D), lambda i, f: (f, 0)),
        ],
        out_specs=pl.BlockSpec((tm, D), lambda i, f: (i, 0)),
        scratch_shapes=[pltpu.VMEM((tm, D), BF16), pltpu.VMEM((tm, D), F32)],
        compiler_params=_cparams(("parallel", "arbitrary")),
        name="ffn",
    )(x, g.reshape(1, D), w_gu, w_gu, w_down)


def _win_body(x_ref, g_ref, w_ref, gate_ref, u_ref):
    n = _rms(x_ref[...], g_ref[...]).astype(BF16)
    y = jnp.dot(n, w_ref[...], preferred_element_type=F32)
    d = gate_ref.shape[1]
    gate_ref[...] = y[:, :d]
    u_ref[...] = y[:, d:]


def _win(x, g, w_in):
    T, D = x.shape
    N = w_in.shape[1] // 2
    tm = min(TOKEN_TILE, T)
    return pl.pallas_call(
        _win_body,
        out_shape=(jax.ShapeDtypeStruct((T, N), F32), jax.ShapeDtypeStruct((T, N), F32)),
        grid=(T // tm,),
        in_specs=[
            pl.BlockSpec((tm, D), lambda i: (i, 0)),
            pl.BlockSpec((1, D), lambda i: (0, 0)),
            pl.BlockSpec((D, 2 * N), lambda i: (0, 0)),
        ],
        out_specs=(pl.BlockSpec((tm, N), lambda i: (i, 0)), pl.BlockSpec((tm, N), lambda i: (i, 0))),
        compiler_params=_cparams(("parallel",)),
        name="rg_in_proj",
    )(x, g.reshape(1, D), w_in)


def _q_body(x_ref, g_ref, w_ref, cos_ref, shi_ref, slo_ref, q_ref):
    n = _rms(x_ref[...], g_ref[...]).astype(BF16)
    q = jnp.dot(n, w_ref[...], preferred_element_type=F32)
    q = _rope(q, cos_ref[...], shi_ref[...], slo_ref[...])
    q_ref[...] = (q * (HEAD_DIM ** -0.5)).astype(q_ref.dtype)


def _table_specs(tm, n_tab_tiles):
    return [pl.BlockSpec((tm, LANES), lambda i: (i % n_tab_tiles, 0))] * 3


def _q_proj(x, g, w_q, tabs, out_dtype):
    T, D = x.shape
    N = w_q.shape[1]
    tm = min(TOKEN_TILE, T)
    return pl.pallas_call(
        _q_body,
        out_shape=jax.ShapeDtypeStruct((T, N), out_dtype),
        grid=(T // tm,),
        in_specs=[
            pl.BlockSpec((tm, D), lambda i: (i, 0)),
            pl.BlockSpec((1, D), lambda i: (0, 0)),
            pl.BlockSpec((D, N), lambda i: (0, 0)),
        ] + _table_specs(tm, tabs[0].shape[0] // tm),
        out_specs=pl.BlockSpec((tm, N), lambda i: (i, 0)),
        compiler_params=_cparams(("parallel",)),
        name="q_proj",
    )(x, g.reshape(1, D), w_q, *tabs)


def _kv_body(x_ref, g_ref, w_ref, cos_ref, shi_ref, slo_ref, k_ref, v_ref, kb_ref, vb_ref):
    n = _rms(x_ref[...], g_ref[...]).astype(BF16)
    kv = jnp.dot(n, w_ref[...], preferred_element_type=F32)
    kd = k_ref.shape[1]
    k = _rope(kv[:, :kd], cos_ref[...], shi_ref[...], slo_ref[...])
    v = kv[:, kd:]
    k_ref[...] = k
    v_ref[...] = v
    kb_ref[...] = k.astype(kb_ref.dtype)
    vb_ref[...] = v.astype(vb_ref.dtype)


def _kv_proj(x, g, w_kv, tabs, kd, lowp_dtype):
    T, D = x.shape
    N = w_kv.shape[1]
    vd = N - kd
    tm = min(TOKEN_TILE, T)
    return pl.pallas_call(
        _kv_body,
        out_shape=(jax.ShapeDtypeStruct((T, kd), F32), jax.ShapeDtypeStruct((T, vd), F32),
                   jax.ShapeDtypeStruct((T, kd), lowp_dtype), jax.ShapeDtypeStruct((T, vd), lowp_dtype)),
        grid=(T // tm,),
        in_specs=[
            pl.BlockSpec((tm, D), lambda i: (i, 0)),
            pl.BlockSpec((1, D), lambda i: (0, 0)),
            pl.BlockSpec((D, N), lambda i: (0, 0)),
        ] + _table_specs(tm, tabs[0].shape[0] // tm),
        out_specs=(pl.BlockSpec((tm, kd), lambda i: (i, 0)), pl.BlockSpec((tm, vd), lambda i: (i, 0)),
                   pl.BlockSpec((tm, kd), lambda i: (i, 0)), pl.BlockSpec((tm, vd), lambda i: (i, 0))),
        compiler_params=_cparams(("parallel",)),
        name="kv_proj",
    )(x, g.reshape(1, D), w_kv, *tabs)


def _ple_body(x_ref, p_ref, g_ref, wg_ref, wp_ref, fin_ref, o_ref, *, final):
    x = x_ref[...]
    n = _rms(x, g_ref[...]).astype(BF16)
    gate = jax.nn.sigmoid(jnp.dot(n, wg_ref[...], preferred_element_type=F32))
    proj = jnp.dot(p_ref[...].astype(BF16), wp_ref[...], preferred_element_type=F32)
    y = x + gate * proj
    if final:
        y = _rms(y, fin_ref[...])
    o_ref[...] = y


def _ple(x, p, g, w_gate, w_proj, fin_g, final):
    T, D = x.shape
    P = p.shape[1]
    tm = min(TOKEN_TILE, T)
    return pl.pallas_call(
        functools.partial(_ple_body, final=final),
        out_shape=jax.ShapeDtypeStruct((T, D), F32),
        grid=(T // tm,),
        in_specs=[
            pl.BlockSpec((tm, D), lambda i: (i, 0)),
            pl.BlockSpec((tm, P), lambda i: (i, 0)),
            pl.BlockSpec((1, D), lambda i: (0, 0)),
            pl.BlockSpec((D, D), lambda i: (0, 0)),
            pl.BlockSpec((P, D), lambda i: (0, 0)),
            pl.BlockSpec((1, D), lambda i: (0, 0)),
        ],
        out_specs=pl.BlockSpec((tm, D), lambda i: (i, 0)),
        compiler_params=_cparams(("parallel",)),
        name="ple_final" if final else "ple",
    )(x, p, g.reshape(1, D), w_gate, w_proj, fin_g.reshape(1, D))


def _proj_res_body(a_ref, w_ref, r_ref, o_ref):
    o_ref[...] = r_ref[...] + jnp.dot(a_ref[...].astype(BF16), w_ref[...], preferred_element_type=F32)


def _proj_residual(a, w, res):
    T, K = a.shape
    N = w.shape[1]
    tm = min(TOKEN_TILE, T)
    return pl.pallas_call(
        _proj_res_body,
        out_shape=jax.ShapeDtypeStruct((T, N), F32),
        grid=(T // tm,),
        in_specs=[
            pl.BlockSpec((tm, K), lambda i: (i, 0)),
            pl.BlockSpec((K, N), lambda i: (0, 0)),
            pl.BlockSpec((tm, N), lambda i: (i, 0)),
        ],
        out_specs=pl.BlockSpec((tm, N), lambda i: (i, 0)),
        compiler_params=_cparams(("parallel",)),
        name="attn_out_proj",
    )(a, w, res)


def _rg_coeffs(conv, wai_ref, ba, bi, sp):
    cb = conv.astype(BF16)
    nb = wai_ref.shape[0]
    blk = conv.shape[1] // nb
    a_parts, b_parts = [], []
    for n in range(nb):
        sl = slice(n * blk, (n + 1) * blk)
        g = jnp.dot(cb[:, sl], wai_ref[n], preferred_element_type=F32)
        r = jax.nn.sigmoid(g[:, :blk] + ba[:, sl])
        i = jax.nn.sigmoid(g[:, blk:] + bi[:, sl])
        log_a = -RG_C * r * sp[:, sl]
        a = jnp.exp(log_a)
        mult = jnp.sqrt(-jnp.tanh(log_a) * (a * a + 1.0))
        a_parts.append(a)
        b_parts.append(mult * (i * conv[:, sl]))
    return jnp.concatenate(a_parts, axis=1), jnp.concatenate(b_parts, axis=1)


def _rg_prompt_body(gate_ref, u_ref, x_ref, cw_ref, cb_ref, wai_ref, ba_ref, bi_ref, lam_ref, wo_ref,
                    o_ref, nconv_ref, nh_ref, ubuf_ref, a_ref, h_ref, hstate_ref):
    s = pl.program_id(1)
    ts, D = u_ref.shape[1], u_ref.shape[2]

    @pl.when(s == 0)
    def _():
        ubuf_ref[0:SUBLANES, :] = jnp.zeros((SUBLANES, D), F32)
        hstate_ref[...] = jnp.zeros_like(hstate_ref)

    u = u_ref[0]
    ubuf_ref[SUBLANES:SUBLANES + ts, :] = u
    cw = cw_ref[...]
    conv = cb_ref[...] + u * cw[CONV_W - 1:CONV_W, :]
    for k in range(1, CONV_W):
        conv = conv + ubuf_ref[SUBLANES - k:SUBLANES - k + ts, :] * cw[CONV_W - 1 - k:CONV_W - k, :]

    sp = _softplus(-lam_ref[...])
    a, b = _rg_coeffs(conv, wai_ref, ba_ref[...], bi_ref[...], sp)
    a_ref[...] = a
    h_ref[...] = b

    row = lax.broadcasted_iota(jnp.int32, (SUBLANES, D), 0)

    def group(gi, h_prev):
        r0 = pl.multiple_of(gi * SUBLANES, SUBLANES)
        ag = a_ref[pl.ds(r0, SUBLANES), :]
        bg = h_ref[pl.ds(r0, SUBLANES), :]
        for d in (1, 2, 4):
            m = row >= d
            a_sh = pltpu.roll(ag, d, 0)
            b_sh = pltpu.roll(bg, d, 0)
            bg = jnp.where(m, ag * b_sh + bg, bg)
            ag = jnp.where(m, ag * a_sh, ag)
        hg = ag * h_prev + bg
        h_ref[pl.ds(r0, SUBLANES), :] = hg
        return jnp.broadcast_to(hg[SUBLANES - 1:SUBLANES, :], (SUBLANES, D))

    h_last = lax.fori_loop(0, ts // SUBLANES, group, hstate_ref[...])
    hstate_ref[...] = h_last
    ubuf_ref[0:SUBLANES, :] = u[ts - SUBLANES:ts, :]

    y = (_gelu_tanh(gate_ref[0]) * h_ref[...]).astype(BF16)
    o_ref[0] = x_ref[0] + jnp.dot(y, wo_ref[...], preferred_element_type=F32)

    @pl.when(s == pl.num_programs(1) - 1)
    def _():
        nconv_ref[0] = u[ts - (CONV_W - 1):ts, :]
        nh_ref[0] = h_last[0:1, :]


def _rg_prompt(gate, u, x, cw, cb, wai, ba, bi, lam, wo):
    B, S, D = u.shape
    ts = min(SEQ_TILE, S)
    row = lambda a: a.reshape(1, D)
    tok = pl.BlockSpec((1, ts, D), lambda b, s: (b, s, 0))
    vec = pl.BlockSpec((1, D), lambda b, s: (0, 0))
    return pl.pallas_call(
        _rg_prompt_body,
        out_shape=(jax.ShapeDtypeStruct((B, S, D), F32),
                   jax.ShapeDtypeStruct((B, CONV_W - 1, D), F32),
                   jax.ShapeDtypeStruct((B, 1, D), F32)),
        grid=(B, S // ts),
        in_specs=[tok, tok, tok,
                  pl.BlockSpec((CONV_W, D), lambda b, s: (0, 0)), vec,
                  pl.BlockSpec(wai.shape, lambda b, s: (0, 0, 0)), vec, vec, vec,
                  pl.BlockSpec((D, D), lambda b, s: (0, 0))],
        out_specs=(tok,
                   pl.BlockSpec((1, CONV_W - 1, D), lambda b, s: (b, 0, 0)),
                   pl.BlockSpec((1, 1, D), lambda b, s: (b, 0, 0))),
        scratch_shapes=[pltpu.VMEM((ts + SUBLANES, D), F32), pltpu.VMEM((ts, D), F32),
                        pltpu.VMEM((ts, D), F32), pltpu.VMEM((SUBLANES, D), F32)],
        compiler_params=_cparams(("parallel", "arbitrary")),
        name="rg_prompt",
    )(gate, u, x, cw, row(cb), wai, row(ba), row(bi), row(lam), wo)


def _rg_sample_body(gate_ref, u_ref, x_ref, cprev_ref, h0_ref, cw_ref, cb_ref, wai_ref, ba_ref, bi_ref,
                    lam_ref, wo_ref, o_ref, nconv_ref, nh_ref):
    n_t = u_ref.shape[0]
    cw = cw_ref[...]
    hist = [cprev_ref[k] for k in range(CONV_W - 1)] + [u_ref[t] for t in range(n_t)]
    sp = _softplus(-lam_ref[...])
    h = h0_ref[...]
    for t in range(n_t):
        conv = cb_ref[...] + hist[t] * cw[0:1, :]
        for k in range(1, CONV_W):
            conv = conv + hist[t + k] * cw[k:k + 1, :]
        a, b = _rg_coeffs(conv, wai_ref, ba_ref[...], bi_ref[...], sp)
        h = a * h + b
        y = (_gelu_tanh(gate_ref[t]) * h).astype(BF16)
        o_ref[t] = x_ref[t] + jnp.dot(y, wo_ref[...], preferred_element_type=F32)
    for k in range(CONV_W - 1):
        nconv_ref[k] = hist[n_t + k]
    nh_ref[...] = h


def _rg_sample(gate, u, x, cprev, h0, cw, cb, wai, ba, bi, lam, wo):
    n_t, Bd, D = u.shape
    row = lambda a: a.reshape(1, D)
    return pl.pallas_call(
        _rg_sample_body,
        out_shape=(jax.ShapeDtypeStruct((n_t, Bd, D), F32),
                   jax.ShapeDtypeStruct((CONV_W - 1, Bd, D), F32),
                   jax.ShapeDtypeStruct((Bd, D), F32)),
        compiler_params=pltpu.CompilerParams(vmem_limit_bytes=VMEM_LIMIT),
        name="rg_sample",
    )(gate, u, x, cprev, h0, cw, row(cb), wai, row(ba), row(bi), row(lam), wo)


def _diff_lambda(lq1, lk1, lq2, lk2, lambda_init):
    return (jnp.exp(jnp.sum(lq1 * lk1, axis=-1, keepdims=True))
            - jnp.exp(jnp.sum(lq2 * lk2, axis=-1, keepdims=True)) + lambda_init)


def _diff_norm(o0, o1, lam, subln, lambda_init):
    d = o0 - lam * o1
    return _rms(d, subln) * (1.0 - lambda_init)


def _flash_body(qi_tab, ki_tab, q_ref, k_ref, v_ref, lq1_ref, lk1_ref, lq2_ref, lk2_ref, sub_ref,
                o_ref, qm_ref, m_ref, l_ref, acc_ref, *, lambda_init):
    p_id = pl.program_id(2)
    qi = qi_tab[p_id]
    ki = ki_tab[p_id]
    tq = q_ref.shape[1]
    tk = k_ref.shape[1]

    @pl.when(ki == 0)
    def _():
        q = q_ref[0]
        lane = lax.broadcasted_iota(jnp.int32, q.shape, 1)
        zero = jnp.zeros_like(q)
        qm_ref[0:tq, :] = jnp.where(lane < HEAD_DIM, q, zero)
        qm_ref[tq:2 * tq, :] = jnp.where(lane < HEAD_DIM, zero, q)
        m_ref[...] = jnp.full_like(m_ref, NEG_BIG)
        l_ref[...] = jnp.zeros_like(l_ref)
        acc_ref[...] = jnp.zeros_like(acc_ref)

    def step(masked):
        s = lax.dot_general(qm_ref[...], k_ref[0], (((1,), (1,)), ((), ())), preferred_element_type=F32)
        if masked:
            r = lax.broadcasted_iota(jnp.int32, s.shape, 0)
            c = lax.broadcasted_iota(jnp.int32, s.shape, 1)
            r = jnp.where(r >= tq, r - tq, r)
            s = jnp.where(c <= r, s, NEG_BIG)
        m_prev = m_ref[...]
        m_new = jnp.maximum(m_prev, jnp.max(s, axis=1, keepdims=True))
        alpha = jnp.exp(m_prev - m_new)
        p = jnp.exp(s - m_new)
        l_ref[...] = alpha * l_ref[...] + jnp.sum(p, axis=1, keepdims=True)
        acc_ref[...] = alpha * acc_ref[...] + jnp.dot(p.astype(BF16), v_ref[0], preferred_element_type=F32)
        m_ref[...] = m_new

    @pl.when(ki < qi)
    def _():
        step(False)

    @pl.when(ki == qi)
    def _():
        step(True)
        o = acc_ref[...] / l_ref[...]
        lam = _diff_lambda(lq1_ref[...], lk1_ref[...], lq2_ref[...], lk2_ref[...], lambda_init)
        o_ref[0] = _diff_norm(o[0:tq], o[tq:2 * tq], lam, sub_ref[...], lambda_init).astype(o_ref.dtype)


def _flash_prompt(q, k, v, lq1, lk1, lq2, lk2, subln, lambda_init):
    B, S, _ = q.shape
    t = min(ATTN_TILE, S)
    nq = S // t
    pairs = [(a, b) for a in range(nq) for b in range(a + 1)]
    qi_tab = jnp.asarray([a for a, _ in pairs], jnp.int32)
    ki_tab = jnp.asarray([b for _, b in pairs], jnp.int32)
    vec = lambda n: pl.BlockSpec((1, n), lambda b, h, p, qt, kt: (0, 0))
    row = lambda a: a.reshape(1, -1)
    return pl.pallas_call(
        functools.partial(_flash_body, lambda_init=lambda_init),
        out_shape=jax.ShapeDtypeStruct(q.shape, BF16),
        grid_spec=pltpu.PrefetchScalarGridSpec(
            num_scalar_prefetch=2,
            grid=(B, N_HEADS, len(pairs)),
            in_specs=[
                pl.BlockSpec((1, t, V_DIM), lambda b, h, p, qt, kt: (b, qt[p], h)),
                pl.BlockSpec((1, t, V_DIM), lambda b, h, p, qt, kt: (b, kt[p], h)),
                pl.BlockSpec((1, t, V_DIM), lambda b, h, p, qt, kt: (b, kt[p], h)),
                vec(HEAD_DIM), vec(HEAD_DIM), vec(HEAD_DIM), vec(HEAD_DIM), vec(V_DIM),
            ],
            out_specs=pl.BlockSpec((1, t, V_DIM), lambda b, h, p, qt, kt: (b, qt[p], h)),
            scratch_shapes=[pltpu.VMEM((2 * t, V_DIM), BF16), pltpu.VMEM((2 * t, 1), F32),
                            pltpu.VMEM((2 * t, 1), F32), pltpu.VMEM((2 * t, V_DIM), F32)],
        ),
        compiler_params=_cparams(("parallel", "parallel", "arbitrary")),
        name="flash_diff_attn",
    )(qi_tab, ki_tab, q, k, v, row(lq1), row(lk1), row(lq2), row(lk2), row(subln))


def _paged_body(pt_ref, q_ref, kn_ref, vn_ref, *rest, n_pp, lambda_init):
    k_refs = rest[:n_pp]
    v_refs = rest[n_pp:2 * n_pp]
    lq1_ref, lk1_ref, lq2_ref, lk2_ref, sub_ref, o_ref, qrep_ref, kpad_ref, vpad_ref, m_ref, l_ref, acc_ref = rest[2 * n_pp:]
    j = pl.program_id(1)
    n_q = q_ref.shape[1]
    n_hc = 2 * N_HEADS
    rows = n_hc * n_q

    def attend(kp, vp, masked):
        s = lax.dot_general(qrep_ref[...], kp, (((1,), (1,)), ((), ())), preferred_element_type=F32)
        if masked:
            r = lax.broadcasted_iota(jnp.int32, s.shape, 0)
            c = lax.broadcasted_iota(jnp.int32, s.shape, 1)
            s = jnp.where(c <= r % n_q, s, NEG_BIG)
        m_prev = m_ref[...]
        m_new = jnp.maximum(m_prev, jnp.max(s, axis=1, keepdims=True))
        alpha = jnp.exp(m_prev - m_new)
        p = jnp.exp(s - m_new)
        l_ref[...] = alpha * l_ref[...] + jnp.sum(p, axis=1, keepdims=True)
        acc_ref[...] = alpha * acc_ref[...] + jnp.dot(p.astype(BF16), vp, preferred_element_type=F32)
        m_ref[...] = m_new

    @pl.when(j == 0)
    def _():
        q = q_ref[0]
        r = lax.broadcasted_iota(jnp.int32, qrep_ref.shape, 0)
        c = lax.broadcasted_iota(jnp.int32, qrep_ref.shape, 1)
        qrep = jnp.zeros(qrep_ref.shape, F32)
        for t in range(n_q):
            qrep = jnp.where(r % n_q == t, jnp.broadcast_to(q[t:t + 1, :], qrep_ref.shape), qrep)
        qrep_ref[...] = jnp.where(c // HEAD_DIM == r // n_q, qrep, 0.0).astype(BF16)
        m_ref[...] = jnp.full_like(m_ref, NEG_BIG)
        l_ref[...] = jnp.zeros_like(l_ref)
        acc_ref[...] = jnp.zeros_like(acc_ref)
        kpad_ref[...] = jnp.zeros_like(kpad_ref)
        vpad_ref[...] = jnp.zeros_like(vpad_ref)
        kpad_ref[0:n_q, :] = kn_ref[0]
        vpad_ref[0:n_q, :] = vn_ref[0]
        attend(kpad_ref[...].astype(BF16), vpad_ref[...].astype(BF16), True)

    for k_ref, v_ref in zip(k_refs, v_refs):
        attend(k_ref[0].astype(BF16), v_ref[0].astype(BF16), False)

    @pl.when(j == pl.num_programs(1) - 1)
    def _():
        o = acc_ref[...] / l_ref[...]
        lam = _diff_lambda(lq1_ref[...], lk1_ref[...], lq2_ref[...], lk2_ref[...], lambda_init)
        for h in range(N_HEADS):
            blk = o[2 * n_q * h:2 * n_q * (h + 1), V_DIM * h:V_DIM * (h + 1)]
            o_ref[0, :, V_DIM * h:V_DIM * (h + 1)] = _diff_norm(
                blk[0:n_q], blk[n_q:2 * n_q], lam, sub_ref[...], lambda_init)


def _paged_attn(q, k_new, v_new, cache_k, cache_v, page_table, lq1, lk1, lq2, lk2, subln, lambda_init):
    Bd, n_q, W = q.shape
    n_pages = page_table.shape[1]
    page = cache_k.shape[1]
    n_pp = PAGES_PER_STEP if n_pages % PAGES_PER_STEP == 0 else 1
    n_steps = n_pages // n_pp
    rows = 2 * N_HEADS * n_q
    pt = page_table.reshape(-1)

    def page_spec(i):
        return pl.BlockSpec((1, page, W), lambda b, j, pt: (pt[b * n_pages + j * n_pp + i], 0, 0))

    tok = pl.BlockSpec((1, n_q, W), lambda b, j, pt: (b, 0, 0))
    vec = lambda n: pl.BlockSpec((1, n), lambda b, j, pt: (0, 0))
    row = lambda a: a.reshape(1, -1)
    return pl.pallas_call(
        functools.partial(_paged_body, n_pp=n_pp, lambda_init=lambda_init),
        out_shape=jax.ShapeDtypeStruct((Bd, n_q, W), F32),
        grid_spec=pltpu.PrefetchScalarGridSpec(
            num_scalar_prefetch=1,
            grid=(Bd, n_steps),
            in_specs=[tok, tok, tok] + [page_spec(i) for i in range(n_pp)] * 2
                     + [vec(HEAD_DIM)] * 4 + [vec(V_DIM)],
            out_specs=tok,
            scratch_shapes=[pltpu.VMEM((rows, W), BF16), pltpu.VMEM((page, W), F32), pltpu.VMEM((page, W), F32),
                            pltpu.VMEM((rows, 1), F32), pltpu.VMEM((rows, 1), F32), pltpu.VMEM((rows, W), F32)],
        ),
        compiler_params=_cparams(("parallel", "arbitrary")),
        name="paged_diff_attn",
    )(pt, q, k_new, v_new, *([cache_k] * n_pp), *([cache_v] * n_pp),
      row(lq1), row(lk1), row(lq2), row(lk2), row(subln))


def _rope_tables(pos):
    half = HEAD_DIM // 2
    lane = jnp.arange(LANES)
    inv = jnp.power(ROPE_THETA, -(lane % half).astype(F32) * 2.0 / HEAD_DIM)
    ang = pos.astype(F32)[:, None] * inv[None, :]
    cos, sin = jnp.cos(ang), jnp.sin(ang)
    upper = (lane % HEAD_DIM) >= half
    return cos, jnp.where(upper, sin, 0.0), jnp.where(upper, 0.0, -sin)


def kernel(x_prompt, x_sample, p_prompt, p_sample, cache_k, cache_v, page_table, state_conv, state_rglru, ffn1_norm, ffn1_w_gu, ffn1_w_down, mix_norm, rg_w_in, rg_conv_w, rg_conv_b, rg_w_a, rg_b_a, rg_w_i, rg_b_i, rg_lambda, rg_w_out, kv_norm, w_kv, attn_w_q, lambda_q1, lambda_k1, lambda_q2, lambda_k2, attn_subln, attn_w_o, ffn2_norm, ffn2_w_gu, ffn2_w_down, ple_norm, ple_w_gate, ple_w_proj, final_norm):
    B, S, D = x_prompt.shape
    Bd, Sd, _ = x_sample.shape
    depth = ffn1_norm.shape[0]
    n_a = rg_w_in.shape[0]
    assert depth == 2 and n_a == 1, "one recurrent layer followed by one attention layer"
    assert D == N_HEADS * V_DIM
    n_pages, page = page_table.shape[1], cache_k.shape[1]
    past_len = n_pages * page
    kd = 2 * N_HEADS * HEAD_DIM
    lambda_init = 0.8 - 0.6 * math.exp(-0.3 * 1)

    bf = lambda w: w.astype(BF16)
    w_gu1, w_dn1, w_gu2, w_dn2 = bf(ffn1_w_gu), bf(ffn1_w_down), bf(ffn2_w_gu), bf(ffn2_w_down)
    w_in, w_out, w_kvb, w_q, w_o = bf(rg_w_in[0]), bf(rg_w_out[0]), bf(w_kv), bf(attn_w_q[0]), bf(attn_w_o[0])
    w_pg, w_pp = bf(ple_w_gate), bf(ple_w_proj)
    wai = bf(jnp.concatenate([rg_w_a[0], rg_w_i[0]], axis=-1))

    def layer0_pre(x):
        x = _ffn(x, ffn1_norm[0], w_gu1[0], w_dn1[0])
        gate, u = _win(x, mix_norm[0], w_in)
        return x, gate, u

    def layer0_post(x, p0, tabs, lowp):
        x = _ffn(x, ffn2_norm[0], w_gu2[0], w_dn2[0])
        x = _ple(x, p0, ple_norm[0], w_pg[0], w_pp[0], final_norm, False)
        k, v, kb, vb = _kv_proj(x, kv_norm, w_kvb, tabs, kd, lowp)
        x = _ffn(x, ffn1_norm[1], w_gu1[1], w_dn1[1])
        q = _q_proj(x, mix_norm[1], w_q, tabs, lowp)
        return x, q, k, v, kb, vb

    def layer1_post(x, dn, p1):
        x = _proj_residual(dn, w_o, x)
        x = _ffn(x, ffn2_norm[1], w_gu2[1], w_dn2[1])
        return _ple(x, p1, ple_norm[1], w_pg[1], w_pp[1], final_norm, True)

    rg_args = (rg_conv_w[0], rg_conv_b[0], wai, rg_b_a[0], rg_b_i[0], rg_lambda[0], w_out)
    lam_args = (lambda_q1[0], lambda_k1[0], lambda_q2[0], lambda_k2[0], attn_subln[0], lambda_init)

    Tp = B * S
    tabs_p = _rope_tables(jnp.arange(S))
    x, gate, u = layer0_pre(x_prompt.reshape(Tp, D))
    x, new_conv_p, new_h_p = _rg_prompt(gate.reshape(B, S, D), u.reshape(B, S, D), x.reshape(B, S, D), *rg_args)
    x, q, k_p, v_p, kb, vb = layer0_post(x.reshape(Tp, D), p_prompt[0].reshape(Tp, -1), tabs_p, BF16)
    dn = _flash_prompt(q.reshape(B, S, D), kb.reshape(B, S, D), vb.reshape(B, S, D), *lam_args)
    y_prompt = layer1_post(x, dn.reshape(Tp, D), p_prompt[1].reshape(Tp, -1)).reshape(B, S, D)

    Ts = Sd * Bd
    tm = lambda a: jnp.swapaxes(a, 0, 1)
    tabs_s = _rope_tables(jnp.repeat(past_len + jnp.arange(Sd), Bd))
    x, gate, u = layer0_pre(tm(x_sample).reshape(Ts, D))
    x, new_conv_s, new_h_s = _rg_sample(gate.reshape(Sd, Bd, D), u.reshape(Sd, Bd, D), x.reshape(Sd, Bd, D),
                                        tm(state_conv[0]), state_rglru[0], *rg_args)
    p_s = jnp.swapaxes(p_sample, 1, 2).reshape(depth, Ts, -1)
    x, q, k_s, v_s, _, _ = layer0_post(x.reshape(Ts, D), p_s[0], tabs_s, F32)
    bm = lambda a: tm(a.reshape(Sd, Bd, -1))
    k_s, v_s = bm(k_s), bm(v_s)
    dn = _paged_attn(bm(q), k_s, v_s, cache_k.reshape(cache_k.shape[0], page, -1),
                     cache_v.reshape(cache_v.shape[0], page, -1), page_table, *lam_args)
    y_sample = bm(layer1_post(x, tm(dn).reshape(Ts, D), p_s[1]))

    return (y_prompt, y_sample,
            k_p.reshape(B, S, 2 * N_HEADS, HEAD_DIM), v_p.reshape(B, S, N_HEADS, V_DIM),
            k_s.reshape(Bd, Sd, 2 * N_HEADS, HEAD_DIM), v_s.reshape(Bd, Sd, N_HEADS, V_DIM),
            new_conv_p[None], new_h_p.reshape(1, B, D),
            tm(new_conv_s)[None], new_h_s[None])
```

```python
import functools
import math

import jax
import jax.numpy as jnp
from jax import lax
from jax.experimental import pallas as pl
from jax.experimental.pallas import tpu as pltpu

F32 = jnp.float32
BF16 = jnp.bfloat16

EPS = 1e-6
N_HEADS = 8
HEAD_DIM = 64
V_DIM = 2 * HEAD_DIM
N_RG_BLOCKS = 8
CONV_W = 4
RG_C = 8.0
ROPE_THETA = 10000.0
LANES = 128
SUBLANES = 8
NEG_BIG = -1e30
VMEM_LIMIT = 56 * 1024 * 1024

TOKEN_TILE = 512
SEQ_TILE = 512
ATTN_CHUNK = 256
ATTN_JOINT = 4
PAGES_PER_STEP = 8


def _cparams(sem):
    return pltpu.CompilerParams(dimension_semantics=sem, vmem_limit_bytes=VMEM_LIMIT)


def _rms(x, g):
    return x * lax.rsqrt(jnp.mean(x * x, axis=-1, keepdims=True) + EPS) * g


def _rope(x, cos, sin_hi, sin_lo):
    outs = []
    for j in range(x.shape[1] // LANES):
        xj = x[:, j * LANES:(j + 1) * LANES]
        outs.append(xj * cos
                    + pltpu.roll(xj, LANES - HEAD_DIM // 2, 1) * sin_lo
                    + pltpu.roll(xj, HEAD_DIM // 2, 1) * sin_hi)
    return jnp.concatenate(outs, axis=1)


def _gelu_tanh(x):
    return 0.5 * x * (1.0 + jnp.tanh(math.sqrt(2.0 / math.pi) * (x + 0.044715 * (x * x * x))))


def _softplus(z):
    return jnp.maximum(z, 0.0) + jnp.log1p(jnp.exp(-jnp.abs(z)))


def _ffn_body(x_ref, g_ref, wg_ref, wu_ref, wd_ref, o_ref, n_ref, acc_ref):
    f = pl.program_id(1)

    @pl.when(f == 0)
    def _():
        n_ref[...] = _rms(x_ref[...], g_ref[...]).astype(BF16)
        acc_ref[...] = jnp.zeros_like(acc_ref)

    n = n_ref[...]
    g = jnp.dot(n, wg_ref[...], preferred_element_type=F32)
    u = jnp.dot(n, wu_ref[...], preferred_element_type=F32)
    h = (g * jax.nn.sigmoid(g) * u).astype(BF16)
    acc_ref[...] += jnp.dot(h, wd_ref[...], preferred_element_type=F32)

    @pl.when(f == pl.num_programs(1) - 1)
    def _():
        o_ref[...] = x_ref[...] + 0.5 * acc_ref[...]


def _ffn_split(d_ff):
    for nf in (2, 4, 1):
        if d_ff % (nf * LANES) == 0:
            return nf
    raise ValueError(f"unsupported FFN width {d_ff}")


def _ffn(x, g, w_gu, w_down):
    T, D = x.shape
    d_ff = w_down.shape[0]
    nf = _ffn_split(d_ff)
    tf = d_ff // nf
    tm = min(TOKEN_TILE, T)
    return pl.pallas_call(
        _ffn_body,
        out_shape=jax.ShapeDtypeStruct((T, D), F32),
        grid=(T // tm, nf),
        in_specs=[
            pl.BlockSpec((tm, D), lambda i, f: (i, 0)),
            pl.BlockSpec((1, D), lambda i, f: (0, 0)),
            pl.BlockSpec((D, tf), lambda i, f: (0, f)),
            pl.BlockSpec((D, tf), lambda i, f: (0, f + nf)),
            pl.BlockSpec((tf, D), lambda i, f: (f, 0)),
        ],
        out_specs=pl.BlockSpec((tm, D), lambda i, f: (i, 0)),
        scratch_shapes=[pltpu.VMEM((tm, D), BF16), pltpu.VMEM((tm, D), F32)],
        compiler_params=_cparams(("parallel", "arbitrary")),
        name="ffn",
    )(x, g.reshape(1, D), w_gu, w_gu, w_down)


def _win_body(x_ref, g_ref, w_ref, gate_ref, u_ref):
    n = _rms(x_ref[...], g_ref[...]).astype(BF16)
    y = jnp.dot(n, w_ref[...], preferred_element_type=F32)
    d = gate_ref.shape[1]
    gate_ref[...] = y[:, :d]
    u_ref[...] = y[:, d:]


def _win(x, g, w_in):
    T, D = x.shape
    N = w_in.shape[1] // 2
    tm = min(TOKEN_TILE, T)
    return pl.pallas_call(
        _win_body,
        out_shape=(jax.ShapeDtypeStruct((T, N), F32), jax.ShapeDtypeStruct((T, N), F32)),
        grid=(T // tm,),
        in_specs=[
            pl.BlockSpec((tm, D), lambda i: (i, 0)),
            pl.BlockSpec((1, D), lambda i: (0, 0)),
            pl.BlockSpec((D, 2 * N), lambda i: (0, 0)),
        ],
        out_specs=(pl.BlockSpec((tm, N), lambda i: (i, 0)), pl.BlockSpec((tm, N), lambda i: (i, 0))),
        compiler_params=_cparams(("parallel",)),
        name="rg_in_proj",
    )(x, g.reshape(1, D), w_in)


def _store_head_transposed(dst_ref, x):
    n_chunks, chunk = dst_ref.shape[2], dst_ref.shape[4]
    for h in range(N_HEADS):
        xt = x[:, h * V_DIM:(h + 1) * V_DIM].T.astype(dst_ref.dtype)
        for c in range(n_chunks):
            dst_ref[0, h, c] = xt[:, c * chunk:(c + 1) * chunk]


def _head_transposed_out(B, S, tm, dtype):
    tiles_per_seq = S // tm
    shape = jax.ShapeDtypeStruct((B, N_HEADS, S // ATTN_CHUNK, V_DIM, ATTN_CHUNK), dtype)
    spec = pl.BlockSpec((1, N_HEADS, tm // ATTN_CHUNK, V_DIM, ATTN_CHUNK),
                        lambda i: (i // tiles_per_seq, 0, i % tiles_per_seq, 0, 0))
    return shape, spec


def _q_body(x_ref, g_ref, w_ref, cos_ref, shi_ref, slo_ref, q_ref, *, transposed):
    n = _rms(x_ref[...], g_ref[...]).astype(BF16)
    q = jnp.dot(n, w_ref[...], preferred_element_type=F32)
    q = _rope(q, cos_ref[...], shi_ref[...], slo_ref[...]) * (HEAD_DIM ** -0.5)
    if transposed:
        _store_head_transposed(q_ref, q)
    else:
        q_ref[...] = q


def _table_specs(tm, n_tab_tiles):
    return [pl.BlockSpec((tm, LANES), lambda i: (i % n_tab_tiles, 0))] * 3


def _q_proj(x, g, w_q, tabs, seq_shape=None):
    T, D = x.shape
    N = w_q.shape[1]
    tm = min(TOKEN_TILE, T)
    if seq_shape is None:
        out_shape, out_spec = jax.ShapeDtypeStruct((T, N), F32), pl.BlockSpec((tm, N), lambda i: (i, 0))
    else:
        out_shape, out_spec = _head_transposed_out(*seq_shape, tm, BF16)
    return pl.pallas_call(
        functools.partial(_q_body, transposed=seq_shape is not None),
        out_shape=out_shape,
        grid=(T // tm,),
        in_specs=[
            pl.BlockSpec((tm, D), lambda i: (i, 0)),
            pl.BlockSpec((1, D), lambda i: (0, 0)),
            pl.BlockSpec((D, N), lambda i: (0, 0)),
        ] + _table_specs(tm, tabs[0].shape[0] // tm),
        out_specs=out_spec,
        compiler_params=_cparams(("parallel",)),
        name="q_proj",
    )(x, g.reshape(1, D), w_q, *tabs)


def _kv_body(x_ref, g_ref, w_ref, cos_ref, shi_ref, slo_ref, k_ref, v_ref, *lowp_refs):
    n = _rms(x_ref[...], g_ref[...]).astype(BF16)
    kv = jnp.dot(n, w_ref[...], preferred_element_type=F32)
    kd = k_ref.shape[1]
    k = _rope(kv[:, :kd], cos_ref[...], shi_ref[...], slo_ref[...])
    v = kv[:, kd:]
    k_ref[...] = k
    v_ref[...] = v
    if lowp_refs:
        kb_ref, vt_ref = lowp_refs
        kb_ref[...] = k.astype(kb_ref.dtype)
        _store_head_transposed(vt_ref, v)


def _kv_proj(x, g, w_kv, tabs, kd, seq_shape=None):
    T, D = x.shape
    N = w_kv.shape[1]
    vd = N - kd
    tm = min(TOKEN_TILE, T)
    out_shape = [jax.ShapeDtypeStruct((T, kd), F32), jax.ShapeDtypeStruct((T, vd), F32)]
    out_specs = [pl.BlockSpec((tm, kd), lambda i: (i, 0)), pl.BlockSpec((tm, vd), lambda i: (i, 0))]
    if seq_shape is not None:
        vt_shape, vt_spec = _head_transposed_out(*seq_shape, tm, BF16)
        out_shape += [jax.ShapeDtypeStruct((T, kd), BF16), vt_shape]
        out_specs += [pl.BlockSpec((tm, kd), lambda i: (i, 0)), vt_spec]
    return pl.pallas_call(
        _kv_body,
        out_shape=tuple(out_shape),
        grid=(T // tm,),
        in_specs=[
            pl.BlockSpec((tm, D), lambda i: (i, 0)),
            pl.BlockSpec((1, D), lambda i: (0, 0)),
            pl.BlockSpec((D, N), lambda i: (0, 0)),
        ] + _table_specs(tm, tabs[0].shape[0] // tm),
        out_specs=tuple(out_specs),
        compiler_params=_cparams(("parallel",)),
        name="kv_proj",
    )(x, g.reshape(1, D), w_kv, *tabs)


def _ple_body(x_ref, p_ref, g_ref, wg_ref, wp_ref, fin_ref, o_ref, *, final):
    x = x_ref[...]
    n = _rms(x, g_ref[...]).astype(BF16)
    gate = jax.nn.sigmoid(jnp.dot(n, wg_ref[...], preferred_element_type=F32))
    proj = jnp.dot(p_ref[...].astype(BF16), wp_ref[...], preferred_element_type=F32)
    y = x + gate * proj
    if final:
        y = _rms(y, fin_ref[...])
    o_ref[...] = y


def _ple(x, p_all, layer, g, w_gate, w_proj, fin_g, final):
    T, D = x.shape
    P = p_all.shape[2]
    tm = min(TOKEN_TILE, T)
    return pl.pallas_call(
        functools.partial(_ple_body, final=final),
        out_shape=jax.ShapeDtypeStruct((T, D), F32),
        grid=(T // tm,),
        in_specs=[
            pl.BlockSpec((tm, D), lambda i: (i, 0)),
            pl.BlockSpec((None, tm, P), lambda i: (layer, i, 0)),
            pl.BlockSpec((1, D), lambda i: (0, 0)),
            pl.BlockSpec((D, D), lambda i: (0, 0)),
            pl.BlockSpec((P, D), lambda i: (0, 0)),
            pl.BlockSpec((1, D), lambda i: (0, 0)),
        ],
        out_specs=pl.BlockSpec((tm, D), lambda i: (i, 0)),
        compiler_params=_cparams(("parallel",)),
        name="ple_final" if final else "ple",
    )(x, p_all, g.reshape(1, D), w_gate, w_proj, fin_g.reshape(1, D))


def _proj_res_body(a_ref, w_ref, r_ref, o_ref):
    o_ref[...] = r_ref[...] + jnp.dot(a_ref[...].astype(BF16), w_ref[...], preferred_element_type=F32)


def _proj_residual(a, w, res):
    T, K = a.shape
    N = w.shape[1]
    tm = min(TOKEN_TILE, T)
    return pl.pallas_call(
        _proj_res_body,
        out_shape=jax.ShapeDtypeStruct((T, N), F32),
        grid=(T // tm,),
        in_specs=[
            pl.BlockSpec((tm, K), lambda i: (i, 0)),
            pl.BlockSpec((K, N), lambda i: (0, 0)),
            pl.BlockSpec((tm, N), lambda i: (i, 0)),
        ],
        out_specs=pl.BlockSpec((tm, N), lambda i: (i, 0)),
        compiler_params=_cparams(("parallel",)),
        name="attn_out_proj",
    )(a, w, res)


def _rg_coeffs(conv, wai_ref, ba, bi, sp):
    cb = conv.astype(BF16)
    nb = wai_ref.shape[0]
    blk = conv.shape[1] // nb
    a_parts, b_parts = [], []
    for n in range(nb):
        sl = slice(n * blk, (n + 1) * blk)
        g = jnp.dot(cb[:, sl], wai_ref[n], preferred_element_type=F32)
        r = jax.nn.sigmoid(g[:, :blk] + ba[:, sl])
        i = jax.nn.sigmoid(g[:, blk:] + bi[:, sl])
        log_a = -RG_C * r * sp[:, sl]
        a = jnp.exp(log_a)
        mult = jnp.sqrt(-jnp.tanh(log_a) * (a * a + 1.0))
        a_parts.append(a)
        b_parts.append(mult * (i * conv[:, sl]))
    return jnp.concatenate(a_parts, axis=1), jnp.concatenate(b_parts, axis=1)


def _rg_prompt_body(gate_ref, u_ref, x_ref, cw_ref, cb_ref, wai_ref, ba_ref, bi_ref, lam_ref, wo_ref,
                    o_ref, nconv_ref, nh_ref, ubuf_ref, a_ref, h_ref, hstate_ref):
    s = pl.program_id(1)
    ts, D = u_ref.shape[1], u_ref.shape[2]

    @pl.when(s == 0)
    def _():
        ubuf_ref[0:SUBLANES, :] = jnp.zeros((SUBLANES, D), F32)
        hstate_ref[...] = jnp.zeros_like(hstate_ref)

    u = u_ref[0]
    ubuf_ref[SUBLANES:SUBLANES + ts, :] = u
    cw = cw_ref[...]
    conv = cb_ref[...] + u * cw[CONV_W - 1:CONV_W, :]
    for k in range(1, CONV_W):
        conv = conv + ubuf_ref[SUBLANES - k:SUBLANES - k + ts, :] * cw[CONV_W - 1 - k:CONV_W - k, :]

    sp = _softplus(-lam_ref[...])
    a, b = _rg_coeffs(conv, wai_ref, ba_ref[...], bi_ref[...], sp)
    a_ref[...] = a
    h_ref[...] = b

    row = lax.broadcasted_iota(jnp.int32, (SUBLANES, D), 0)

    def group(gi, h_prev):
        r0 = pl.multiple_of(gi * SUBLANES, SUBLANES)
        ag = a_ref[pl.ds(r0, SUBLANES), :]
        bg = h_ref[pl.ds(r0, SUBLANES), :]
        for d in (1, 2, 4):
            m = row >= d
            a_sh = pltpu.roll(ag, d, 0)
            b_sh = pltpu.roll(bg, d, 0)
            bg = jnp.where(m, ag * b_sh + bg, bg)
            ag = jnp.where(m, ag * a_sh, ag)
        hg = ag * h_prev + bg
        h_ref[pl.ds(r0, SUBLANES), :] = hg
        return jnp.broadcast_to(hg[SUBLANES - 1:SUBLANES, :], (SUBLANES, D))

    h_last = lax.fori_loop(0, ts // SUBLANES, group, hstate_ref[...])
    hstate_ref[...] = h_last
    ubuf_ref[0:SUBLANES, :] = u[ts - SUBLANES:ts, :]

    y = (_gelu_tanh(gate_ref[0]) * h_ref[...]).astype(BF16)
    o_ref[0] = x_ref[0] + jnp.dot(y, wo_ref[...], preferred_element_type=F32)

    @pl.when(s == pl.num_programs(1) - 1)
    def _():
        nconv_ref[0] = u[ts - (CONV_W - 1):ts, :]
        nh_ref[0] = h_last[0:1, :]


def _rg_prompt(gate, u, x, cw, cb, wai, ba, bi, lam, wo):
    B, S, D = u.shape
    ts = min(SEQ_TILE, S)
    row = lambda a: a.reshape(1, D)
    tok = pl.BlockSpec((1, ts, D), lambda b, s: (b, s, 0))
    vec = pl.BlockSpec((1, D), lambda b, s: (0, 0))
    return pl.pallas_call(
        _rg_prompt_body,
        out_shape=(jax.ShapeDtypeStruct((B, S, D), F32),
                   jax.ShapeDtypeStruct((B, CONV_W - 1, D), F32),
                   jax.ShapeDtypeStruct((B, 1, D), F32)),
        grid=(B, S // ts),
        in_specs=[tok, tok, tok,
                  pl.BlockSpec((CONV_W, D), lambda b, s: (0, 0)), vec,
                  pl.BlockSpec(wai.shape, lambda b, s: (0, 0, 0)), vec, vec, vec,
                  pl.BlockSpec((D, D), lambda b, s: (0, 0))],
        out_specs=(tok,
                   pl.BlockSpec((1, CONV_W - 1, D), lambda b, s: (b, 0, 0)),
                   pl.BlockSpec((1, 1, D), lambda b, s: (b, 0, 0))),
        scratch_shapes=[pltpu.VMEM((ts + SUBLANES, D), F32), pltpu.VMEM((ts, D), F32),
                        pltpu.VMEM((ts, D), F32), pltpu.VMEM((SUBLANES, D), F32)],
        compiler_params=_cparams(("parallel", "arbitrary")),
        name="rg_prompt",
    )(gate, u, x, cw, row(cb), wai, row(ba), row(bi), row(lam), wo)


def _rg_sample_body(gate_ref, u_ref, x_ref, cprev_ref, h0_ref, cw_ref, cb_ref, wai_ref, ba_ref, bi_ref,
                    lam_ref, wo_ref, o_ref, nconv_ref, nh_ref):
    n_t = u_ref.shape[0]
    cw = cw_ref[...]
    hist = [cprev_ref[k] for k in range(CONV_W - 1)] + [u_ref[t] for t in range(n_t)]
    sp = _softplus(-lam_ref[...])
    h = h0_ref[...]
    for t in range(n_t):
        conv = cb_ref[...] + hist[t] * cw[0:1, :]
        for k in range(1, CONV_W):
            conv = conv + hist[t + k] * cw[k:k + 1, :]
        a, b = _rg_coeffs(conv, wai_ref, ba_ref[...], bi_ref[...], sp)
        h = a * h + b
        y = (_gelu_tanh(gate_ref[t]) * h).astype(BF16)
        o_ref[t] = x_ref[t] + jnp.dot(y, wo_ref[...], preferred_element_type=F32)
    for k in range(CONV_W - 1):
        nconv_ref[k] = hist[n_t + k]
    nh_ref[...] = h


def _rg_sample(gate, u, x, cprev, h0, cw, cb, wai, ba, bi, lam, wo):
    n_t, Bd, D = u.shape
    row = lambda a: a.reshape(1, D)
    return pl.pallas_call(
        _rg_sample_body,
        out_shape=(jax.ShapeDtypeStruct((n_t, Bd, D), F32),
                   jax.ShapeDtypeStruct((CONV_W - 1, Bd, D), F32),
                   jax.ShapeDtypeStruct((Bd, D), F32)),
        compiler_params=pltpu.CompilerParams(vmem_limit_bytes=VMEM_LIMIT),
        name="rg_sample",
    )(gate, u, x, cprev, h0, cw, row(cb), wai, row(ba), row(bi), row(lam), wo)


def _diff_lambda(lq1, lk1, lq2, lk2, lambda_init):
    return (jnp.exp(jnp.sum(lq1 * lk1, axis=-1, keepdims=True))
            - jnp.exp(jnp.sum(lq2 * lk2, axis=-1, keepdims=True)) + lambda_init)


def _diff_norm(o0, o1, lam, subln, lambda_init):
    d = o0 - lam * o1
    return _rms(d, subln) * (1.0 - lambda_init)


def _flash_body(qt_ref, k_ref, vt_ref, lq1_ref, lk1_ref, lq2_ref, lk2_ref, sub_ref, o_ref, w_ref, acc_ref,
                m_ref, l_ref, *, lambda_init):
    i = pl.program_id(2)
    tq = qt_ref.shape[4]
    chunk = vt_ref.shape[4]

    qt = qt_ref[0, 0, 0]
    feat = lax.broadcasted_iota(jnp.int32, qt.shape, 0)
    zero = jnp.zeros_like(qt)
    w_ref[:, 0:tq] = jnp.where(feat < HEAD_DIM, qt, zero)
    w_ref[:, tq:2 * tq] = jnp.where(feat < HEAD_DIM, zero, qt)
    acc_ref[...] = jnp.zeros_like(acc_ref)
    m_ref[...] = jnp.full_like(m_ref, NEG_BIG)
    l_ref[...] = jnp.zeros_like(l_ref)

    def attend(first, n_full, diagonal):
        cs = [first + t for t in range(n_full)] + ([i] if diagonal else [])
        s = []
        for t, c in enumerate(cs):
            r0 = pl.multiple_of(c * chunk, chunk)
            st = jnp.dot(k_ref[0, pl.ds(r0, chunk), :], w_ref[...], preferred_element_type=F32)
            if diagonal and t == n_full:
                key = lax.broadcasted_iota(jnp.int32, st.shape, 0)
                qry = lax.broadcasted_iota(jnp.int32, st.shape, 1)
                qry = jnp.where(qry >= tq, qry - tq, qry)
                st = jnp.where(key <= qry, st, NEG_BIG)
            s.append(st)
        m_prev = m_ref[...]
        m_new = m_prev
        for st in s:
            m_new = jnp.maximum(m_new, jnp.max(st, axis=0, keepdims=True))
        alpha = jnp.exp(m_prev - m_new)
        l_new = alpha * l_ref[...]
        pv = None
        for st, c in zip(s, cs):
            p = jnp.exp(st - m_new)
            l_new = l_new + jnp.sum(p, axis=0, keepdims=True)
            part = jnp.dot(vt_ref[0, 0, c], p.astype(BF16), preferred_element_type=F32)
            pv = part if pv is None else pv + part
        acc_ref[...] = alpha * acc_ref[...] + pv
        m_ref[...] = m_new
        l_ref[...] = l_new

    def main(g, carry):
        attend(g * ATTN_JOINT, ATTN_JOINT, False)
        return carry

    lax.fori_loop(0, i // ATTN_JOINT, main, 0)
    for r in range(ATTN_JOINT):
        @pl.when(i % ATTN_JOINT == r)
        def _():
            attend(i - r, r, True)

    o = acc_ref[...] / l_ref[...]
    lam = _diff_lambda(lq1_ref[...], lk1_ref[...], lq2_ref[...], lk2_ref[...], lambda_init)
    d = (o[:, 0:tq] - lam * o[:, tq:2 * tq]).T
    o_ref[0] = (_rms(d, sub_ref[...]) * (1.0 - lambda_init)).astype(o_ref.dtype)


def _flash_prompt(qt, kb, vt, lq1, lk1, lq2, lk2, subln, lambda_init):
    B, S, W = kb.shape
    n_chunks, chunk = qt.shape[2], qt.shape[4]
    vec = lambda n: pl.BlockSpec((1, n), lambda b, h, i: (0, 0))
    row = lambda a: a.reshape(1, -1)
    return pl.pallas_call(
        functools.partial(_flash_body, lambda_init=lambda_init),
        out_shape=jax.ShapeDtypeStruct((B, S, W), BF16),
        grid=(B, N_HEADS, n_chunks),
        in_specs=[
            pl.BlockSpec((1, 1, 1, V_DIM, chunk), lambda b, h, i: (b, h, i, 0, 0)),
            pl.BlockSpec((1, S, V_DIM), lambda b, h, i: (b, 0, h)),
            pl.BlockSpec((1, 1, n_chunks, V_DIM, chunk), lambda b, h, i: (b, h, 0, 0, 0)),
            vec(HEAD_DIM), vec(HEAD_DIM), vec(HEAD_DIM), vec(HEAD_DIM), vec(V_DIM),
        ],
        out_specs=pl.BlockSpec((1, chunk, V_DIM), lambda b, h, i: (b, i, h)),
        scratch_shapes=[pltpu.VMEM((V_DIM, 2 * chunk), BF16), pltpu.VMEM((V_DIM, 2 * chunk), F32),
                        pltpu.VMEM((1, 2 * chunk), F32), pltpu.VMEM((1, 2 * chunk), F32)],
        compiler_params=_cparams(("parallel", "parallel", "arbitrary")),
        name="flash_diff_attn",
    )(qt, kb, vt, row(lq1), row(lk1), row(lq2), row(lk2), row(subln))


def _paged_body(pt_ref, q_ref, kn_ref, vn_ref, *rest, n_pp, n_q, lambda_init):
    k_refs = rest[:n_pp]
    v_refs = rest[n_pp:2 * n_pp]
    lq1_ref, lk1_ref, lq2_ref, lk2_ref, sub_ref, o_ref, bias_ref, m_ref, l_ref, acc_ref = rest[2 * n_pp:]
    j = pl.program_id(1)
    half = N_HEADS * n_q

    def attend(k_of, v, bias):
        s = [lax.dot_general(q_ref[0, c], k_of(c), (((1,), (1,)), ((), ())), preferred_element_type=F32)
             for c in range(2)]
        s = jnp.concatenate(s, axis=0) + bias
        m_prev = m_ref[...]
        m_new = jnp.maximum(m_prev, jnp.max(s, axis=1, keepdims=True))
        alpha = jnp.exp(m_prev - m_new)
        p = jnp.exp(s - m_new)
        l_ref[...] = alpha * l_ref[...] + jnp.sum(p, axis=1, keepdims=True)
        acc_ref[...] = alpha * acc_ref[...] + jnp.dot(p, v, preferred_element_type=F32)
        m_ref[...] = m_new

    @pl.when(j == 0)
    def _():
        r = lax.broadcasted_iota(jnp.int32, bias_ref.shape, 0)
        c = lax.broadcasted_iota(jnp.int32, bias_ref.shape, 1)
        same_head = (c % N_HEADS) == ((r % half) // n_q)
        bias_ref[...] = jnp.where(same_head, 0.0, NEG_BIG)
        m_ref[...] = jnp.full_like(m_ref, NEG_BIG)
        l_ref[...] = jnp.zeros_like(l_ref)
        acc_ref[...] = jnp.zeros_like(acc_ref)
        n_new = kn_ref.shape[2]
        rn = lax.broadcasted_iota(jnp.int32, (2 * half, n_new), 0)
        cn = lax.broadcasted_iota(jnp.int32, (2 * half, n_new), 1)
        ok = ((cn % N_HEADS) == ((rn % half) // n_q)) & ((cn // N_HEADS) <= (rn % n_q))
        attend(lambda c: kn_ref[0, c], vn_ref[0], jnp.where(ok, 0.0, NEG_BIG))

    for k_ref, v_ref in zip(k_refs, v_refs):
        n_keys = v_ref.shape[1]
        attend(lambda c: k_ref[0, pl.ds(c, n_keys, stride=2), :], v_ref[0], bias_ref[...])

    @pl.when(j == pl.num_programs(1) - 1)
    def _():
        o = acc_ref[...] / l_ref[...]
        lam = _diff_lambda(lq1_ref[...], lk1_ref[...], lq2_ref[...], lk2_ref[...], lambda_init)
        o_ref[0] = _diff_norm(o[0:half], o[half:2 * half], lam, sub_ref[...], lambda_init)


def _paged_attn(q, k_new, v_new, cache_k, cache_v, page_table, lq1, lk1, lq2, lk2, subln, lambda_init):
    Bd, _, half, _ = q.shape
    n_q = half // N_HEADS
    n_pool, page = cache_k.shape[0], cache_k.shape[1]
    n_pages = page_table.shape[1]
    n_pp = next(n for n in (PAGES_PER_STEP, 4, 2, 1) if n_pages % n == 0)
    n_steps = n_pages // n_pp
    ck = cache_k.reshape(n_pool, page * 2 * N_HEADS, HEAD_DIM)
    cv = cache_v.reshape(n_pool, page * N_HEADS, V_DIM)
    pt = page_table.reshape(-1)

    def page_spec(rows, width, i):
        return pl.BlockSpec((1, rows, width), lambda b, j, pt: (pt[b * n_pages + j * n_pp + i], 0, 0))

    per_seq = lambda a: pl.BlockSpec((1,) + a.shape[1:], lambda b, j, pt: (b,) + (0,) * (a.ndim - 1))
    vec = lambda n: pl.BlockSpec((1, n), lambda b, j, pt: (0, 0))
    row = lambda a: a.reshape(1, -1)
    return pl.pallas_call(
        functools.partial(_paged_body, n_pp=n_pp, n_q=n_q, lambda_init=lambda_init),
        out_shape=jax.ShapeDtypeStruct((Bd, half, V_DIM), F32),
        grid_spec=pltpu.PrefetchScalarGridSpec(
            num_scalar_prefetch=1,
            grid=(Bd, n_steps),
            in_specs=[per_seq(q), per_seq(k_new), per_seq(v_new)]
                     + [page_spec(ck.shape[1], HEAD_DIM, i) for i in range(n_pp)]
                     + [page_spec(cv.shape[1], V_DIM, i) for i in range(n_pp)]
                     + [vec(HEAD_DIM)] * 4 + [vec(V_DIM)],
            out_specs=pl.BlockSpec((1, half, V_DIM), lambda b, j, pt: (b, 0, 0)),
            scratch_shapes=[pltpu.VMEM((2 * half, page * N_HEADS), F32), pltpu.VMEM((2 * half, 1), F32),
                            pltpu.VMEM((2 * half, 1), F32), pltpu.VMEM((2 * half, V_DIM), F32)],
        ),
        compiler_params=_cparams(("parallel", "arbitrary")),
        name="paged_diff_attn",
    )(pt, q, k_new, v_new, *([ck] * n_pp), *([cv] * n_pp), row(lq1), row(lk1), row(lq2), row(lk2), row(subln))


def _rope_tables(pos):
    half = HEAD_DIM // 2
    lane = jnp.arange(LANES)
    inv = jnp.power(ROPE_THETA, -(lane % half).astype(F32) * 2.0 / HEAD_DIM)
    ang = pos.astype(F32)[:, None] * inv[None, :]
    cos, sin = jnp.cos(ang), jnp.sin(ang)
    upper = (lane % HEAD_DIM) >= half
    return cos, jnp.where(upper, sin, 0.0), jnp.where(upper, 0.0, -sin)


def kernel(x_prompt, x_sample, p_prompt, p_sample, cache_k, cache_v, page_table, state_conv, state_rglru, ffn1_norm, ffn1_w_gu, ffn1_w_down, mix_norm, rg_w_in, rg_conv_w, rg_conv_b, rg_w_a, rg_b_a, rg_w_i, rg_b_i, rg_lambda, rg_w_out, kv_norm, w_kv, attn_w_q, lambda_q1, lambda_k1, lambda_q2, lambda_k2, attn_subln, attn_w_o, ffn2_norm, ffn2_w_gu, ffn2_w_down, ple_norm, ple_w_gate, ple_w_proj, final_norm):
    B, S, D = x_prompt.shape
    Bd, Sd, _ = x_sample.shape
    depth = ffn1_norm.shape[0]
    n_a = rg_w_in.shape[0]
    assert depth == 2 and n_a == 1, "one recurrent layer followed by one attention layer"
    assert D == N_HEADS * V_DIM
    n_pages, page = page_table.shape[1], cache_k.shape[1]
    past_len = n_pages * page
    kd = 2 * N_HEADS * HEAD_DIM
    lambda_init = 0.8 - 0.6 * math.exp(-0.3 * 1)

    bf = lambda w: w.astype(BF16)
    w_gu1, w_dn1, w_gu2, w_dn2 = bf(ffn1_w_gu), bf(ffn1_w_down), bf(ffn2_w_gu), bf(ffn2_w_down)
    w_in, w_out, w_kvb, w_q, w_o = bf(rg_w_in[0]), bf(rg_w_out[0]), bf(w_kv), bf(attn_w_q[0]), bf(attn_w_o[0])
    w_pg, w_pp = bf(ple_w_gate), bf(ple_w_proj)
    wai = bf(jnp.concatenate([rg_w_a[0], rg_w_i[0]], axis=-1))

    def layer0_pre(x):
        x = _ffn(x, ffn1_norm[0], w_gu1[0], w_dn1[0])
        gate, u = _win(x, mix_norm[0], w_in)
        return x, gate, u

    def layer0_post(x, p_all, tabs, seq_shape):
        x = _ffn(x, ffn2_norm[0], w_gu2[0], w_dn2[0])
        x = _ple(x, p_all, 0, ple_norm[0], w_pg[0], w_pp[0], final_norm, False)
        kv = _kv_proj(x, kv_norm, w_kvb, tabs, kd, seq_shape)
        x = _ffn(x, ffn1_norm[1], w_gu1[1], w_dn1[1])
        q = _q_proj(x, mix_norm[1], w_q, tabs, seq_shape)
        return x, q, kv

    def layer1_post(x, dn, p_all):
        x = _proj_residual(dn, w_o, x)
        x = _ffn(x, ffn2_norm[1], w_gu2[1], w_dn2[1])
        return _ple(x, p_all, 1, ple_norm[1], w_pg[1], w_pp[1], final_norm, True)

    rg_args = (rg_conv_w[0], rg_conv_b[0], wai, rg_b_a[0], rg_b_i[0], rg_lambda[0], w_out)
    lam_args = (lambda_q1[0], lambda_k1[0], lambda_q2[0], lambda_k2[0], attn_subln[0], lambda_init)

    Tp = B * S
    tabs_p = _rope_tables(jnp.arange(S))
    p_p = p_prompt.reshape(depth, Tp, -1)
    x, gate, u = layer0_pre(x_prompt.reshape(Tp, D))
    x, new_conv_p, new_h_p = _rg_prompt(gate.reshape(B, S, D), u.reshape(B, S, D), x.reshape(B, S, D), *rg_args)
    x, qt, (k_p, v_p, kb, vt) = layer0_post(x.reshape(Tp, D), p_p, tabs_p, (B, S))
    dn = _flash_prompt(qt, kb.reshape(B, S, D), vt, *lam_args)
    y_prompt = layer1_post(x, dn.reshape(Tp, D), p_p).reshape(B, S, D)

    Ts = Sd * Bd
    tm = lambda a: jnp.swapaxes(a, 0, 1)
    tabs_s = _rope_tables(jnp.repeat(past_len + jnp.arange(Sd), Bd))
    p_s = jnp.swapaxes(p_sample, 1, 2).reshape(depth, Ts, -1)
    x, gate, u = layer0_pre(tm(x_sample).reshape(Ts, D))
    x, new_conv_s, new_h_s = _rg_sample(gate.reshape(Sd, Bd, D), u.reshape(Sd, Bd, D), x.reshape(Sd, Bd, D),
                                        tm(state_conv[0]), state_rglru[0], *rg_args)
    x, q, (k_s, v_s) = layer0_post(x.reshape(Ts, D), p_s, tabs_s, None)
    q5 = q.reshape(Sd, Bd, N_HEADS, 2, HEAD_DIM)
    k5 = k_s.reshape(Sd, Bd, N_HEADS, 2, HEAD_DIM)
    dn = _paged_attn(q5.transpose(1, 3, 2, 0, 4).reshape(Bd, 2, N_HEADS * Sd, HEAD_DIM),
                     k5.transpose(1, 3, 0, 2, 4).reshape(Bd, 2, Sd * N_HEADS, HEAD_DIM),
                     tm(v_s.reshape(Sd, Bd, D)).reshape(Bd, Sd * N_HEADS, V_DIM),
                     cache_k, cache_v, page_table, *lam_args)
    dn = dn.reshape(Bd, N_HEADS, Sd, V_DIM).transpose(2, 0, 1, 3).reshape(Ts, D)
    bm = lambda a: tm(a.reshape(Sd, Bd, -1))
    y_sample = bm(layer1_post(x, dn, p_s))

    return (y_prompt, y_sample,
            k_p.reshape(B, S, 2 * N_HEADS, HEAD_DIM), v_p.reshape(B, S, N_HEADS, V_DIM),
            bm(k_s).reshape(Bd, Sd, 2 * N_HEADS, HEAD_DIM), bm(v_s).reshape(Bd, Sd, N_HEADS, V_DIM),
            new_conv_p[None], new_h_p.reshape(1, B, D),
            tm(new_conv_s)[None], new_h_s[None])
```

```python
import functools
import math

import jax
import jax.numpy as jnp
from jax import lax
from jax.experimental import pallas as pl
from jax.experimental.pallas import tpu as pltpu

F32 = jnp.float32
BF16 = jnp.bfloat16

EPS = 1e-6
N_HEADS = 8
HEAD_DIM = 64
V_DIM = 2 * HEAD_DIM
N_RG_BLOCKS = 8
CONV_W = 4
RG_C = 8.0
ROPE_THETA = 10000.0
LANES = 128
SUBLANES = 8
NEG_BIG = -1e30
VMEM_LIMIT = 56 * 1024 * 1024

TOKEN_TILE = 512
SEQ_TILE = 512
ATTN_CHUNK = 256
ATTN_JOINT = 4
PAGES_PER_STEP = 8


def _cparams(sem):
    return pltpu.CompilerParams(dimension_semantics=sem, vmem_limit_bytes=VMEM_LIMIT)


def _rms(x, g):
    return x * lax.rsqrt(jnp.mean(x * x, axis=-1, keepdims=True) + EPS) * g


def _rope(x, cos, sin_hi, sin_lo):
    outs = []
    for j in range(x.shape[1] // LANES):
        xj = x[:, j * LANES:(j + 1) * LANES]
        outs.append(xj * cos
                    + pltpu.roll(xj, LANES - HEAD_DIM // 2, 1) * sin_lo
                    + pltpu.roll(xj, HEAD_DIM // 2, 1) * sin_hi)
    return jnp.concatenate(outs, axis=1)


def _gelu_tanh(x):
    return 0.5 * x * (1.0 + jnp.tanh(math.sqrt(2.0 / math.pi) * (x + 0.044715 * (x * x * x))))


def _softplus(z):
    return jnp.maximum(z, 0.0) + jnp.log1p(jnp.exp(-jnp.abs(z)))


def _ffn_body(x_ref, g_ref, wg_ref, wu_ref, wd_ref, o_ref, n_ref, acc_ref):
    f = pl.program_id(1)

    @pl.when(f == 0)
    def _():
        n_ref[...] = _rms(x_ref[...], g_ref[...]).astype(BF16)
        acc_ref[...] = jnp.zeros_like(acc_ref)

    n = n_ref[...]
    g = jnp.dot(n, wg_ref[...], preferred_element_type=F32)
    u = jnp.dot(n, wu_ref[...], preferred_element_type=F32)
    h = (g * jax.nn.sigmoid(g) * u).astype(BF16)
    acc_ref[...] += jnp.dot(h, wd_ref[...], preferred_element_type=F32)

    @pl.when(f == pl.num_programs(1) - 1)
    def _():
        o_ref[...] = x_ref[...] + 0.5 * acc_ref[...]


def _ffn_split(d_ff):
    for nf in (2, 4, 1):
        if d_ff % (nf * LANES) == 0:
            return nf
    raise ValueError(f"unsupported FFN width {d_ff}")


def _ffn(x, g, w_gu, w_down):
    T, D = x.shape
    d_ff = w_down.shape[0]
    nf = _ffn_split(d_ff)
    tf = d_ff // nf
    tm = min(TOKEN_TILE, T)
    return pl.pallas_call(
        _ffn_body,
        out_shape=jax.ShapeDtypeStruct((T, D), F32),
        grid=(T // tm, nf),
        in_specs=[
            pl.BlockSpec((tm, D), lambda i, f: (i, 0)),
            pl.BlockSpec((1, D), lambda i, f: (0, 0)),
            pl.BlockSpec((D, tf), lambda i, f: (0, f)),
            pl.BlockSpec((D, tf), lambda i, f: (0, f + nf)),
            pl.BlockSpec((tf, D), lambda i, f: (f, 0)),
        ],
        out_specs=pl.BlockSpec((tm, D), lambda i, f: (i, 0)),
        scratch_shapes=[pltpu.VMEM((tm, D), BF16), pltpu.VMEM((tm, D), F32)],
        compiler_params=_cparams(("parallel", "arbitrary")),
        name="ffn",
    )(x, g.reshape(1, D), w_gu, w_gu, w_down)


def _win_body(x_ref, g_ref, w_ref, gate_ref, u_ref):
    n = _rms(x_ref[...], g_ref[...]).astype(BF16)
    y = jnp.dot(n, w_ref[...], preferred_element_type=F32)
    d = gate_ref.shape[1]
    gate_ref[...] = y[:, :d]
    u_ref[...] = y[:, d:]


def _win(x, g, w_in):
    T, D = x.shape
    N = w_in.shape[1] // 2
    tm = min(TOKEN_TILE, T)
    return pl.pallas_call(
        _win_body,
        out_shape=(jax.ShapeDtypeStruct((T, N), F32), jax.ShapeDtypeStruct((T, N), F32)),
        grid=(T // tm,),
        in_specs=[
            pl.BlockSpec((tm, D), lambda i: (i, 0)),
            pl.BlockSpec((1, D), lambda i: (0, 0)),
            pl.BlockSpec((D, 2 * N), lambda i: (0, 0)),
        ],
        out_specs=(pl.BlockSpec((tm, N), lambda i: (i, 0)), pl.BlockSpec((tm, N), lambda i: (i, 0))),
        compiler_params=_cparams(("parallel",)),
        name="rg_in_proj",
    )(x, g.reshape(1, D), w_in)


def _store_head_transposed(dst_ref, x):
    n_chunks, chunk = dst_ref.shape[2], dst_ref.shape[4]
    for h in range(N_HEADS):
        xt = x[:, h * V_DIM:(h + 1) * V_DIM].T.astype(dst_ref.dtype)
        for c in range(n_chunks):
            dst_ref[0, h, c] = xt[:, c * chunk:(c + 1) * chunk]


def _head_transposed_out(B, S, tm, dtype):
    tiles_per_seq = S // tm
    shape = jax.ShapeDtypeStruct((B, N_HEADS, S // ATTN_CHUNK, V_DIM, ATTN_CHUNK), dtype)
    spec = pl.BlockSpec((1, N_HEADS, tm // ATTN_CHUNK, V_DIM, ATTN_CHUNK),
                        lambda i: (i // tiles_per_seq, 0, i % tiles_per_seq, 0, 0))
    return shape, spec


def _q_body(x_ref, g_ref, w_ref, cos_ref, shi_ref, slo_ref, q_ref, *, transposed):
    n = _rms(x_ref[...], g_ref[...]).astype(BF16)
    q = jnp.dot(n, w_ref[...], preferred_element_type=F32)
    q = _rope(q, cos_ref[...], shi_ref[...], slo_ref[...]) * (HEAD_DIM ** -0.5)
    if transposed:
        _store_head_transposed(q_ref, q)
    else:
        q_ref[...] = q


def _table_specs(tm, n_tab_tiles):
    return [pl.BlockSpec((tm, LANES), lambda i: (i % n_tab_tiles, 0))] * 3


def _q_proj(x, g, w_q, tabs, seq_shape=None):
    T, D = x.shape
    N = w_q.shape[1]
    tm = min(TOKEN_TILE, T)
    if seq_shape is None:
        out_shape, out_spec = jax.ShapeDtypeStruct((T, N), F32), pl.BlockSpec((tm, N), lambda i: (i, 0))
    else:
        out_shape, out_spec = _head_transposed_out(*seq_shape, tm, BF16)
    return pl.pallas_call(
        functools.partial(_q_body, transposed=seq_shape is not None),
        out_shape=out_shape,
        grid=(T // tm,),
        in_specs=[
            pl.BlockSpec((tm, D), lambda i: (i, 0)),
            pl.BlockSpec((1, D), lambda i: (0, 0)),
            pl.BlockSpec((D, N), lambda i: (0, 0)),
        ] + _table_specs(tm, tabs[0].shape[0] // tm),
        out_specs=out_spec,
        compiler_params=_cparams(("parallel",)),
        name="q_proj",
    )(x, g.reshape(1, D), w_q, *tabs)


def _kv_body(x_ref, g_ref, w_ref, cos_ref, shi_ref, slo_ref, k_ref, v_ref, *lowp_refs):
    n = _rms(x_ref[...], g_ref[...]).astype(BF16)
    kv = jnp.dot(n, w_ref[...], preferred_element_type=F32)
    kd = N_HEADS * V_DIM
    k = _rope(kv[:, :kd], cos_ref[...], shi_ref[...], slo_ref[...])
    v = kv[:, kd:]
    if not lowp_refs:
        k_ref[...] = k
        v_ref[...] = v
    else:
        for h in range(N_HEADS):
            k_ref[0, h] = k[:, h * V_DIM:(h + 1) * V_DIM].T
        v_ref[...] = v
        kb_ref, vt_ref = lowp_refs
        kb_ref[...] = k.astype(kb_ref.dtype)
        _store_head_transposed(vt_ref, v)


def _kv_proj(x, g, w_kv, tabs, kd, seq_shape=None):
    T, D = x.shape
    N = w_kv.shape[1]
    vd = N - kd
    tm = min(TOKEN_TILE, T)
    out_shape = [jax.ShapeDtypeStruct((T, kd), F32), jax.ShapeDtypeStruct((T, vd), F32)]
    out_specs = [pl.BlockSpec((tm, kd), lambda i: (i, 0)), pl.BlockSpec((tm, vd), lambda i: (i, 0))]
    if seq_shape is not None:
        B, S = seq_shape
        tiles_per_seq = S // tm
        out_shape[0] = jax.ShapeDtypeStruct((B, N_HEADS, V_DIM, S), F32)
        out_specs[0] = pl.BlockSpec((1, N_HEADS, V_DIM, tm), lambda i: (i // tiles_per_seq, 0, 0, i % tiles_per_seq))
        vt_shape, vt_spec = _head_transposed_out(B, S, tm, BF16)
        out_shape += [jax.ShapeDtypeStruct((T, kd), BF16), vt_shape]
        out_specs += [pl.BlockSpec((tm, kd), lambda i: (i, 0)), vt_spec]
    return pl.pallas_call(
        _kv_body,
        out_shape=tuple(out_shape),
        grid=(T // tm,),
        in_specs=[
            pl.BlockSpec((tm, D), lambda i: (i, 0)),
            pl.BlockSpec((1, D), lambda i: (0, 0)),
            pl.BlockSpec((D, N), lambda i: (0, 0)),
        ] + _table_specs(tm, tabs[0].shape[0] // tm),
        out_specs=tuple(out_specs),
        compiler_params=_cparams(("parallel",)),
        name="kv_proj",
    )(x, g.reshape(1, D), w_kv, *tabs)


def _ple_body(x_ref, p_ref, g_ref, wg_ref, wp_ref, fin_ref, o_ref, *, final):
    x = x_ref[...]
    n = _rms(x, g_ref[...]).astype(BF16)
    gate = jax.nn.sigmoid(jnp.dot(n, wg_ref[...], preferred_element_type=F32))
    proj = jnp.dot(p_ref[...].astype(BF16), wp_ref[...], preferred_element_type=F32)
    y = x + gate * proj
    if final:
        y = _rms(y, fin_ref[...])
    o_ref[...] = y


def _ple(x, p_all, layer, g, w_gate, w_proj, fin_g, final):
    T, D = x.shape
    P = p_all.shape[2]
    tm = min(TOKEN_TILE, T)
    return pl.pallas_call(
        functools.partial(_ple_body, final=final),
        out_shape=jax.ShapeDtypeStruct((T, D), F32),
        grid=(T // tm,),
        in_specs=[
            pl.BlockSpec((tm, D), lambda i: (i, 0)),
            pl.BlockSpec((None, tm, P), lambda i: (layer, i, 0)),
            pl.BlockSpec((1, D), lambda i: (0, 0)),
            pl.BlockSpec((D, D), lambda i: (0, 0)),
            pl.BlockSpec((P, D), lambda i: (0, 0)),
            pl.BlockSpec((1, D), lambda i: (0, 0)),
        ],
        out_specs=pl.BlockSpec((tm, D), lambda i: (i, 0)),
        compiler_params=_cparams(("parallel",)),
        name="ple_final" if final else "ple",
    )(x, p_all, g.reshape(1, D), w_gate, w_proj, fin_g.reshape(1, D))


def _proj_res_body(a_ref, w_ref, r_ref, o_ref):
    o_ref[...] = r_ref[...] + jnp.dot(a_ref[...].astype(BF16), w_ref[...], preferred_element_type=F32)


def _proj_residual(a, w, res):
    T, K = a.shape
    N = w.shape[1]
    tm = min(TOKEN_TILE, T)
    return pl.pallas_call(
        _proj_res_body,
        out_shape=jax.ShapeDtypeStruct((T, N), F32),
        grid=(T // tm,),
        in_specs=[
            pl.BlockSpec((tm, K), lambda i: (i, 0)),
            pl.BlockSpec((K, N), lambda i: (0, 0)),
            pl.BlockSpec((tm, N), lambda i: (i, 0)),
        ],
        out_specs=pl.BlockSpec((tm, N), lambda i: (i, 0)),
        compiler_params=_cparams(("parallel",)),
        name="attn_out_proj",
    )(a, w, res)


def _rg_coeffs(conv, wai_ref, ba, bi, sp):
    cb = conv.astype(BF16)
    nb = wai_ref.shape[0]
    blk = conv.shape[1] // nb
    a_parts, b_parts = [], []
    for n in range(nb):
        sl = slice(n * blk, (n + 1) * blk)
        g = jnp.dot(cb[:, sl], wai_ref[n], preferred_element_type=F32)
        r = jax.nn.sigmoid(g[:, :blk] + ba[:, sl])
        i = jax.nn.sigmoid(g[:, blk:] + bi[:, sl])
        log_a = -RG_C * r * sp[:, sl]
        a = jnp.exp(log_a)
        mult = jnp.sqrt(-jnp.tanh(log_a) * (a * a + 1.0))
        a_parts.append(a)
        b_parts.append(mult * (i * conv[:, sl]))
    return jnp.concatenate(a_parts, axis=1), jnp.concatenate(b_parts, axis=1)


def _rg_prompt_body(gate_ref, u_ref, x_ref, cw_ref, cb_ref, wai_ref, ba_ref, bi_ref, lam_ref, wo_ref,
                    o_ref, nconv_ref, nh_ref, ubuf_ref, a_ref, h_ref, hstate_ref):
    s = pl.program_id(1)
    ts, D = u_ref.shape[1], u_ref.shape[2]

    @pl.when(s == 0)
    def _():
        ubuf_ref[0:SUBLANES, :] = jnp.zeros((SUBLANES, D), F32)
        hstate_ref[...] = jnp.zeros_like(hstate_ref)

    u = u_ref[0]
    ubuf_ref[SUBLANES:SUBLANES + ts, :] = u
    cw = cw_ref[...]
    conv = cb_ref[...] + u * cw[CONV_W - 1:CONV_W, :]
    for k in range(1, CONV_W):
        conv = conv + ubuf_ref[SUBLANES - k:SUBLANES - k + ts, :] * cw[CONV_W - 1 - k:CONV_W - k, :]

    sp = _softplus(-lam_ref[...])
    a, b = _rg_coeffs(conv, wai_ref, ba_ref[...], bi_ref[...], sp)
    a_ref[...] = a
    h_ref[...] = b

    row = lax.broadcasted_iota(jnp.int32, (SUBLANES, D), 0)

    def group(gi, h_prev):
        r0 = pl.multiple_of(gi * SUBLANES, SUBLANES)
        ag = a_ref[pl.ds(r0, SUBLANES), :]
        bg = h_ref[pl.ds(r0, SUBLANES), :]
        for d in (1, 2, 4):
            m = row >= d
            a_sh = pltpu.roll(ag, d, 0)
            b_sh = pltpu.roll(bg, d, 0)
            bg = jnp.where(m, ag * b_sh + bg, bg)
            ag = jnp.where(m, ag * a_sh, ag)
        hg = ag * h_prev + bg
        h_ref[pl.ds(r0, SUBLANES), :] = hg
        return jnp.broadcast_to(hg[SUBLANES - 1:SUBLANES, :], (SUBLANES, D))

    h_last = lax.fori_loop(0, ts // SUBLANES, group, hstate_ref[...])
    hstate_ref[...] = h_last
    ubuf_ref[0:SUBLANES, :] = u[ts - SUBLANES:ts, :]

    y = (_gelu_tanh(gate_ref[0]) * h_ref[...]).astype(BF16)
    o_ref[0] = x_ref[0] + jnp.dot(y, wo_ref[...], preferred_element_type=F32)

    @pl.when(s == pl.num_programs(1) - 1)
    def _():
        nconv_ref[0] = u[ts - (CONV_W - 1):ts, :]
        nh_ref[0] = h_last[0:1, :]


def _rg_prompt(gate, u, x, cw, cb, wai, ba, bi, lam, wo):
    B, S, D = u.shape
    ts = min(SEQ_TILE, S)
    row = lambda a: a.reshape(1, D)
    tok = pl.BlockSpec((1, ts, D), lambda b, s: (b, s, 0))
    vec = pl.BlockSpec((1, D), lambda b, s: (0, 0))
    return pl.pallas_call(
        _rg_prompt_body,
        out_shape=(jax.ShapeDtypeStruct((B, S, D), F32),
                   jax.ShapeDtypeStruct((B, CONV_W - 1, D), F32),
                   jax.ShapeDtypeStruct((B, 1, D), F32)),
        grid=(B, S // ts),
        in_specs=[tok, tok, tok,
                  pl.BlockSpec((CONV_W, D), lambda b, s: (0, 0)), vec,
                  pl.BlockSpec(wai.shape, lambda b, s: (0, 0, 0)), vec, vec, vec,
                  pl.BlockSpec((D, D), lambda b, s: (0, 0))],
        out_specs=(tok,
                   pl.BlockSpec((1, CONV_W - 1, D), lambda b, s: (b, 0, 0)),
                   pl.BlockSpec((1, 1, D), lambda b, s: (b, 0, 0))),
        scratch_shapes=[pltpu.VMEM((ts + SUBLANES, D), F32), pltpu.VMEM((ts, D), F32),
                        pltpu.VMEM((ts, D), F32), pltpu.VMEM((SUBLANES, D), F32)],
        compiler_params=_cparams(("parallel", "arbitrary")),
        name="rg_prompt",
    )(gate, u, x, cw, row(cb), wai, row(ba), row(bi), row(lam), wo)


def _rg_sample_body(gate_ref, u_ref, x_ref, cprev_ref, h0_ref, cw_ref, cb_ref, wai_ref, ba_ref, bi_ref,
                    lam_ref, wo_ref, o_ref, nconv_ref, nh_ref):
    n_t = u_ref.shape[0]
    cw = cw_ref[...]
    hist = [cprev_ref[k] for k in range(CONV_W - 1)] + [u_ref[t] for t in range(n_t)]
    sp = _softplus(-lam_ref[...])
    h = h0_ref[...]
    for t in range(n_t):
        conv = cb_ref[...] + hist[t] * cw[0:1, :]
        for k in range(1, CONV_W):
            conv = conv + hist[t + k] * cw[k:k + 1, :]
        a, b = _rg_coeffs(conv, wai_ref, ba_ref[...], bi_ref[...], sp)
        h = a * h + b
        y = (_gelu_tanh(gate_ref[t]) * h).astype(BF16)
        o_ref[t] = x_ref[t] + jnp.dot(y, wo_ref[...], preferred_element_type=F32)
    for k in range(CONV_W - 1):
        nconv_ref[k] = hist[n_t + k]
    nh_ref[...] = h


def _rg_sample(gate, u, x, cprev, h0, cw, cb, wai, ba, bi, lam, wo):
    n_t, Bd, D = u.shape
    row = lambda a: a.reshape(1, D)
    return pl.pallas_call(
        _rg_sample_body,
        out_shape=(jax.ShapeDtypeStruct((n_t, Bd, D), F32),
                   jax.ShapeDtypeStruct((CONV_W - 1, Bd, D), F32),
                   jax.ShapeDtypeStruct((Bd, D), F32)),
        compiler_params=pltpu.CompilerParams(vmem_limit_bytes=VMEM_LIMIT),
        name="rg_sample",
    )(gate, u, x, cprev, h0, cw, row(cb), wai, row(ba), row(bi), row(lam), wo)


def _diff_lambda(lq1, lk1, lq2, lk2, lambda_init):
    return (jnp.exp(jnp.sum(lq1 * lk1, axis=-1, keepdims=True))
            - jnp.exp(jnp.sum(lq2 * lk2, axis=-1, keepdims=True)) + lambda_init)


def _diff_norm(o0, o1, lam, subln, lambda_init):
    d = o0 - lam * o1
    return _rms(d, subln) * (1.0 - lambda_init)


def _flash_body(qt_ref, k_ref, vt_ref, lq1_ref, lk1_ref, lq2_ref, lk2_ref, sub_ref, o_ref, w_ref, acc_ref,
                m_ref, l_ref, *, lambda_init):
    i = pl.program_id(2)
    tq = qt_ref.shape[4]
    chunk = vt_ref.shape[4]

    qt = qt_ref[0, 0, 0]
    feat = lax.broadcasted_iota(jnp.int32, qt.shape, 0)
    zero = jnp.zeros_like(qt)
    w_ref[:, 0:tq] = jnp.where(feat < HEAD_DIM, qt, zero)
    w_ref[:, tq:2 * tq] = jnp.where(feat < HEAD_DIM, zero, qt)
    acc_ref[...] = jnp.zeros_like(acc_ref)
    m_ref[...] = jnp.full_like(m_ref, NEG_BIG)
    l_ref[...] = jnp.zeros_like(l_ref)

    def attend(first, n_full, diagonal):
        cs = [first + t for t in range(n_full)] + ([i] if diagonal else [])
        s = []
        for t, c in enumerate(cs):
            r0 = pl.multiple_of(c * chunk, chunk)
            st = jnp.dot(k_ref[0, pl.ds(r0, chunk), :], w_ref[...], preferred_element_type=F32)
            if diagonal and t == n_full:
                key = lax.broadcasted_iota(jnp.int32, st.shape, 0)
                qry = lax.broadcasted_iota(jnp.int32, st.shape, 1)
                qry = jnp.where(qry >= tq, qry - tq, qry)
                st = jnp.where(key <= qry, st, NEG_BIG)
            s.append(st)
        m_prev = m_ref[...]
        m_new = m_prev
        for st in s:
            m_new = jnp.maximum(m_new, jnp.max(st, axis=0, keepdims=True))
        alpha = jnp.exp(m_prev - m_new)
        l_new = alpha * l_ref[...]
        pv = None
        for st, c in zip(s, cs):
            p = jnp.exp(st - m_new)
            l_new = l_new + jnp.sum(p, axis=0, keepdims=True)
            part = jnp.dot(vt_ref[0, 0, c], p.astype(BF16), preferred_element_type=F32)
            pv = part if pv is None else pv + part
        acc_ref[...] = alpha * acc_ref[...] + pv
        m_ref[...] = m_new
        l_ref[...] = l_new

    def main(g, carry):
        attend(g * ATTN_JOINT, ATTN_JOINT, False)
        return carry

    lax.fori_loop(0, i // ATTN_JOINT, main, 0)
    for r in range(ATTN_JOINT):
        @pl.when(i % ATTN_JOINT == r)
        def _():
            attend(i - r, r, True)

    o = acc_ref[...] / l_ref[...]
    lam = _diff_lambda(lq1_ref[...], lk1_ref[...], lq2_ref[...], lk2_ref[...], lambda_init)
    d = (o[:, 0:tq] - lam * o[:, tq:2 * tq]).T
    o_ref[0] = (_rms(d, sub_ref[...]) * (1.0 - lambda_init)).astype(o_ref.dtype)


def _flash_prompt(qt, kb, vt, lq1, lk1, lq2, lk2, subln, lambda_init):
    B, S, W = kb.shape
    n_chunks, chunk = qt.shape[2], qt.shape[4]
    vec = lambda n: pl.BlockSpec((1, n), lambda b, h, i: (0, 0))
    row = lambda a: a.reshape(1, -1)
    return pl.pallas_call(
        functools.partial(_flash_body, lambda_init=lambda_init),
        out_shape=jax.ShapeDtypeStruct((B, S, W), BF16),
        grid=(B, N_HEADS, n_chunks),
        in_specs=[
            pl.BlockSpec((1, 1, 1, V_DIM, chunk), lambda b, h, i: (b, h, i, 0, 0)),
            pl.BlockSpec((1, S, V_DIM), lambda b, h, i: (b, 0, h)),
            pl.BlockSpec((1, 1, n_chunks, V_DIM, chunk), lambda b, h, i: (b, h, 0, 0, 0)),
            vec(HEAD_DIM), vec(HEAD_DIM), vec(HEAD_DIM), vec(HEAD_DIM), vec(V_DIM),
        ],
        out_specs=pl.BlockSpec((1, chunk, V_DIM), lambda b, h, i: (b, i, h)),
        scratch_shapes=[pltpu.VMEM((V_DIM, 2 * chunk), BF16), pltpu.VMEM((V_DIM, 2 * chunk), F32),
                        pltpu.VMEM((1, 2 * chunk), F32), pltpu.VMEM((1, 2 * chunk), F32)],
        compiler_params=_cparams(("parallel", "parallel", "arbitrary")),
        name="flash_diff_attn",
    )(qt, kb, vt, row(lq1), row(lk1), row(lq2), row(lk2), row(subln))


def _paged_body(pt_ref, q_ref, kn_ref, vn_ref, *rest, n_pp, lambda_init):
    kt_refs = rest[:n_pp]
    v_refs = rest[n_pp:2 * n_pp]
    lq1_ref, lk1_ref, lq2_ref, lk2_ref, sub_ref, o_ref, qblk_ref, m_ref, l_ref, acc_ref = rest[2 * n_pp:]
    j = pl.program_id(1)
    n_q = q_ref.shape[1]
    page = kt_refs[0].shape[2]
    grp = 2 * n_q

    def update(s, head_pv):
        m_prev = m_ref[...]
        m_new = jnp.maximum(m_prev, jnp.max(s, axis=1, keepdims=True))
        alpha = jnp.exp(m_prev - m_new)
        p = jnp.exp(s - m_new)
        l_ref[...] = alpha * l_ref[...] + jnp.sum(p, axis=1, keepdims=True)
        m_ref[...] = m_new
        for h in range(N_HEADS):
            sl = slice(grp * h, grp * (h + 1))
            acc_ref[sl, :] = alpha[sl] * acc_ref[sl, :] + head_pv(p[sl], h)

    @pl.when(j == 0)
    def _():
        q = q_ref[0]
        r = lax.broadcasted_iota(jnp.int32, qblk_ref.shape, 0)
        c = lax.broadcasted_iota(jnp.int32, qblk_ref.shape, 1)
        qrep = jnp.zeros(qblk_ref.shape, F32)
        for t in range(n_q):
            qrep = jnp.where(r % n_q == t, jnp.broadcast_to(q[t:t + 1, :], qblk_ref.shape), qrep)
        qblk_ref[...] = jnp.where(c // HEAD_DIM == r // n_q, qrep, 0.0)
        m_ref[...] = jnp.full_like(m_ref, NEG_BIG)
        l_ref[...] = jnp.zeros_like(l_ref)
        acc_ref[...] = jnp.zeros_like(acc_ref)
        kn, vn = kn_ref[0], vn_ref[0]
        s = lax.dot_general(qblk_ref[...], kn, (((1,), (1,)), ((), ())), preferred_element_type=F32)
        rn = lax.broadcasted_iota(jnp.int32, s.shape, 0)
        cn = lax.broadcasted_iota(jnp.int32, s.shape, 1)
        s = jnp.where(cn <= rn % n_q, s, NEG_BIG)
        update(s, lambda p, h: jnp.dot(p, vn[:, V_DIM * h:V_DIM * (h + 1)], preferred_element_type=F32))

    qblk = qblk_ref[...]
    s = jnp.concatenate([jnp.dot(qblk, kt_ref[0], preferred_element_type=F32) for kt_ref in kt_refs], axis=1)

    def head_pv(p, h):
        out = None
        for g, v_ref in enumerate(v_refs):
            part = jnp.dot(p[:, page * g:page * (g + 1)], v_ref[0, pl.ds(h, page, stride=N_HEADS), :],
                           preferred_element_type=F32)
            out = part if out is None else out + part
        return out

    update(s, head_pv)

    @pl.when(j == pl.num_programs(1) - 1)
    def _():
        o = acc_ref[...] / l_ref[...]
        lam = _diff_lambda(lq1_ref[...], lk1_ref[...], lq2_ref[...], lk2_ref[...], lambda_init)
        for h in range(N_HEADS):
            blk = o[grp * h:grp * (h + 1)]
            o_ref[0, n_q * h:n_q * (h + 1), :] = _diff_norm(blk[0:n_q], blk[n_q:grp], lam, sub_ref[...], lambda_init)


def _paged_attn(q, k_new, v_new, cache_k, cache_v, page_table, lq1, lk1, lq2, lk2, subln, lambda_init):
    Bd, n_q, W = q.shape
    n_pool, page = cache_k.shape[0], cache_k.shape[1]
    n_pages = page_table.shape[1]
    n_pp = next(n for n in (PAGES_PER_STEP, 4, 2, 1) if n_pages % n == 0)
    n_steps = n_pages // n_pp
    rows = 2 * N_HEADS * n_q
    ckt = cache_k.transpose(0, 2, 3, 1).reshape(n_pool, W, page)
    cv = cache_v.reshape(n_pool, page * N_HEADS, V_DIM)
    pt = page_table.reshape(-1)

    def page_spec(shape, i):
        return pl.BlockSpec((1,) + shape[1:], lambda b, j, pt: (pt[b * n_pages + j * n_pp + i], 0, 0))

    per_seq = lambda a: pl.BlockSpec((1,) + a.shape[1:], lambda b, j, pt: (b, 0, 0))
    vec = lambda n: pl.BlockSpec((1, n), lambda b, j, pt: (0, 0))
    row = lambda a: a.reshape(1, -1)
    return pl.pallas_call(
        functools.partial(_paged_body, n_pp=n_pp, lambda_init=lambda_init),
        out_shape=jax.ShapeDtypeStruct((Bd, N_HEADS * n_q, V_DIM), F32),
        grid_spec=pltpu.PrefetchScalarGridSpec(
            num_scalar_prefetch=1,
            grid=(Bd, n_steps),
            in_specs=[per_seq(q), per_seq(k_new), per_seq(v_new)]
                     + [page_spec(ckt.shape, i) for i in range(n_pp)]
                     + [page_spec(cv.shape, i) for i in range(n_pp)]
                     + [vec(HEAD_DIM)] * 4 + [vec(V_DIM)],
            out_specs=pl.BlockSpec((1, N_HEADS * n_q, V_DIM), lambda b, j, pt: (b, 0, 0)),
            scratch_shapes=[pltpu.VMEM((rows, W), F32), pltpu.VMEM((rows, 1), F32),
                            pltpu.VMEM((rows, 1), F32), pltpu.VMEM((rows, V_DIM), F32)],
        ),
        compiler_params=_cparams(("parallel", "arbitrary")),
        name="paged_diff_attn",
    )(pt, q, k_new, v_new, *([ckt] * n_pp), *([cv] * n_pp), row(lq1), row(lk1), row(lq2), row(lk2), row(subln))


def _old_paged_body(pt_ref, q_ref, kn_ref, vn_ref, *rest, n_pp, n_q, lambda_init):
    k_refs = rest[:n_pp]
    v_refs = rest[n_pp:2 * n_pp]
    lq1_ref, lk1_ref, lq2_ref, lk2_ref, sub_ref, o_ref, bias_ref, m_ref, l_ref, acc_ref = rest[2 * n_pp:]
    j = pl.program_id(1)
    half = N_HEADS * n_q

    def attend(k_of, v, bias):
        s = [lax.dot_general(q_ref[0, c], k_of(c), (((1,), (1,)), ((), ())), preferred_element_type=F32)
             for c in range(2)]
        s = jnp.concatenate(s, axis=0) + bias
        m_prev = m_ref[...]
        m_new = jnp.maximum(m_prev, jnp.max(s, axis=1, keepdims=True))
        alpha = jnp.exp(m_prev - m_new)
        p = jnp.exp(s - m_new)
        l_ref[...] = alpha * l_ref[...] + jnp.sum(p, axis=1, keepdims=True)
        acc_ref[...] = alpha * acc_ref[...] + jnp.dot(p, v, preferred_element_type=F32)
        m_ref[...] = m_new

    @pl.when(j == 0)
    def _():
        r = lax.broadcasted_iota(jnp.int32, bias_ref.shape, 0)
        c = lax.broadcasted_iota(jnp.int32, bias_ref.shape, 1)
        same_head = (c % N_HEADS) == ((r % half) // n_q)
        bias_ref[...] = jnp.where(same_head, 0.0, NEG_BIG)
        m_ref[...] = jnp.full_like(m_ref, NEG_BIG)
        l_ref[...] = jnp.zeros_like(l_ref)
        acc_ref[...] = jnp.zeros_like(acc_ref)
        n_new = kn_ref.shape[2]
        rn = lax.broadcasted_iota(jnp.int32, (2 * half, n_new), 0)
        cn = lax.broadcasted_iota(jnp.int32, (2 * half, n_new), 1)
        ok = ((cn % N_HEADS) == ((rn % half) // n_q)) & ((cn // N_HEADS) <= (rn % n_q))
        attend(lambda c: kn_ref[0, c], vn_ref[0], jnp.where(ok, 0.0, NEG_BIG))

    for k_ref, v_ref in zip(k_refs, v_refs):
        n_keys = v_ref.shape[1]
        attend(lambda c: k_ref[0, pl.ds(c, n_keys, stride=2), :], v_ref[0], bias_ref[...])

    @pl.when(j == pl.num_programs(1) - 1)
    def _():
        o = acc_ref[...] / l_ref[...]
        lam = _diff_lambda(lq1_ref[...], lk1_ref[...], lq2_ref[...], lk2_ref[...], lambda_init)
        o_ref[0] = _diff_norm(o[0:half], o[half:2 * half], lam, sub_ref[...], lambda_init)


def _old_paged_attn(q, k_new, v_new, cache_k, cache_v, page_table, lq1, lk1, lq2, lk2, subln, lambda_init):
    Bd, _, half, _ = q.shape
    n_q = half // N_HEADS
    n_pool, page = cache_k.shape[0], cache_k.shape[1]
    n_pages = page_table.shape[1]
    n_pp = next(n for n in (PAGES_PER_STEP, 4, 2, 1) if n_pages % n == 0)
    n_steps = n_pages // n_pp
    ck = cache_k.reshape(n_pool, page * 2 * N_HEADS, HEAD_DIM)
    cv = cache_v.reshape(n_pool, page * N_HEADS, V_DIM)
    pt = page_table.reshape(-1)

    def page_spec(rows, width, i):
        return pl.BlockSpec((1, rows, width), lambda b, j, pt: (pt[b * n_pages + j * n_pp + i], 0, 0))

    per_seq = lambda a: pl.BlockSpec((1,) + a.shape[1:], lambda b, j, pt: (b,) + (0,) * (a.ndim - 1))
    vec = lambda n: pl.BlockSpec((1, n), lambda b, j, pt: (0, 0))
    row = lambda a: a.reshape(1, -1)
    return pl.pallas_call(
        functools.partial(_paged_body, n_pp=n_pp, n_q=n_q, lambda_init=lambda_init),
        out_shape=jax.ShapeDtypeStruct((Bd, half, V_DIM), F32),
        grid_spec=pltpu.PrefetchScalarGridSpec(
            num_scalar_prefetch=1,
            grid=(Bd, n_steps),
            in_specs=[per_seq(q), per_seq(k_new), per_seq(v_new)]
                     + [page_spec(ck.shape[1], HEAD_DIM, i) for i in range(n_pp)]
                     + [page_spec(cv.shape[1], V_DIM, i) for i in range(n_pp)]
                     + [vec(HEAD_DIM)] * 4 + [vec(V_DIM)],
            out_specs=pl.BlockSpec((1, half, V_DIM), lambda b, j, pt: (b, 0, 0)),
            scratch_shapes=[pltpu.VMEM((2 * half, page * N_HEADS), F32), pltpu.VMEM((2 * half, 1), F32),
                            pltpu.VMEM((2 * half, 1), F32), pltpu.VMEM((2 * half, V_DIM), F32)],
        ),
        compiler_params=_cparams(("parallel", "arbitrary")),
        name="paged_diff_attn",
    )(pt, q, k_new, v_new, *([ck] * n_pp), *([cv] * n_pp), row(lq1), row(lk1), row(lq2), row(lk2), row(subln))


def _rope_tables(pos):
    half = HEAD_DIM // 2
    lane = jnp.arange(LANES)
    inv = jnp.power(ROPE_THETA, -(lane % half).astype(F32) * 2.0 / HEAD_DIM)
    ang = pos.astype(F32)[:, None] * inv[None, :]
    cos, sin = jnp.cos(ang), jnp.sin(ang)
    upper = (lane % HEAD_DIM) >= half
    return cos, jnp.where(upper, sin, 0.0), jnp.where(upper, 0.0, -sin)


def kernel(x_prompt, x_sample, p_prompt, p_sample, cache_k, cache_v, page_table, state_conv, state_rglru, ffn1_norm, ffn1_w_gu, ffn1_w_down, mix_norm, rg_w_in, rg_conv_w, rg_conv_b, rg_w_a, rg_b_a, rg_w_i, rg_b_i, rg_lambda, rg_w_out, kv_norm, w_kv, attn_w_q, lambda_q1, lambda_k1, lambda_q2, lambda_k2, attn_subln, attn_w_o, ffn2_norm, ffn2_w_gu, ffn2_w_down, ple_norm, ple_w_gate, ple_w_proj, final_norm):
    B, S, D = x_prompt.shape
    Bd, Sd, _ = x_sample.shape
    depth = ffn1_norm.shape[0]
    n_a = rg_w_in.shape[0]
    assert depth == 2 and n_a == 1, "one recurrent layer followed by one attention layer"
    assert D == N_HEADS * V_DIM
    n_pages, page = page_table.shape[1], cache_k.shape[1]
    past_len = n_pages * page
    kd = 2 * N_HEADS * HEAD_DIM
    lambda_init = 0.8 - 0.6 * math.exp(-0.3 * 1)

    bf = lambda w: w.astype(BF16)
    w_gu1, w_dn1, w_gu2, w_dn2 = bf(ffn1_w_gu), bf(ffn1_w_down), bf(ffn2_w_gu), bf(ffn2_w_down)
    w_in, w_out, w_kvb, w_q, w_o = bf(rg_w_in[0]), bf(rg_w_out[0]), bf(w_kv), bf(attn_w_q[0]), bf(attn_w_o[0])
    w_pg, w_pp = bf(ple_w_gate), bf(ple_w_proj)
    wai = bf(jnp.concatenate([rg_w_a[0], rg_w_i[0]], axis=-1))

    def layer0_pre(x):
        x = _ffn(x, ffn1_norm[0], w_gu1[0], w_dn1[0])
        gate, u = _win(x, mix_norm[0], w_in)
        return x, gate, u

    def layer0_post(x, p_all, tabs, seq_shape):
        x = _ffn(x, ffn2_norm[0], w_gu2[0], w_dn2[0])
        x = _ple(x, p_all, 0, ple_norm[0], w_pg[0], w_pp[0], final_norm, False)
        kv = _kv_proj(x, kv_norm, w_kvb, tabs, kd, seq_shape)
        x = _ffn(x, ffn1_norm[1], w_gu1[1], w_dn1[1])
        q = _q_proj(x, mix_norm[1], w_q, tabs, seq_shape)
        return x, q, kv

    def layer1_post(x, dn, p_all):
        x = _proj_residual(dn, w_o, x)
        x = _ffn(x, ffn2_norm[1], w_gu2[1], w_dn2[1])
        return _ple(x, p_all, 1, ple_norm[1], w_pg[1], w_pp[1], final_norm, True)

    rg_args = (rg_conv_w[0], rg_conv_b[0], wai, rg_b_a[0], rg_b_i[0], rg_lambda[0], w_out)
    lam_args = (lambda_q1[0], lambda_k1[0], lambda_q2[0], lambda_k2[0], attn_subln[0], lambda_init)

    Tp = B * S
    tabs_p = _rope_tables(jnp.arange(S))
    p_p = p_prompt.reshape(depth, Tp, -1)
    x, gate, u = layer0_pre(x_prompt.reshape(Tp, D))
    x, new_conv_p, new_h_p = _rg_prompt(gate.reshape(B, S, D), u.reshape(B, S, D), x.reshape(B, S, D), *rg_args)
    x, qt, (k_p, v_p, kb, vt) = layer0_post(x.reshape(Tp, D), p_p, tabs_p, (B, S))
    dn = _flash_prompt(qt, kb.reshape(B, S, D), vt, *lam_args)
    y_prompt = layer1_post(x, dn.reshape(Tp, D), p_p).reshape(B, S, D)

    Ts = Sd * Bd
    tm = lambda a: jnp.swapaxes(a, 0, 1)
    tabs_s = _rope_tables(jnp.repeat(past_len + jnp.arange(Sd), Bd))
    p_s = jnp.swapaxes(p_sample, 1, 2).reshape(depth, Ts, -1)
    x, gate, u = layer0_pre(tm(x_sample).reshape(Ts, D))
    x, new_conv_s, new_h_s = _rg_sample(gate.reshape(Sd, Bd, D), u.reshape(Sd, Bd, D), x.reshape(Sd, Bd, D),
                                        tm(state_conv[0]), state_rglru[0], *rg_args)
    x, q, (k_s, v_s) = layer0_post(x.reshape(Ts, D), p_s, tabs_s, None)
    bm = lambda a: tm(a.reshape(Sd, Bd, -1))
    k_s, v_s = bm(k_s), bm(v_s)
    pad = lambda a: jnp.pad(a, ((0, 0), (0, -Sd % SUBLANES), (0, 0)))
    dn = _paged_attn(bm(q), pad(k_s), pad(v_s), cache_k, cache_v, page_table, *lam_args)
    dn = dn.reshape(Bd, N_HEADS, Sd, V_DIM).transpose(2, 0, 1, 3).reshape(Ts, D)
    y_sample = bm(layer1_post(x, dn, p_s))

    return (y_prompt, y_sample,
            k_p.reshape(B, 2 * N_HEADS, HEAD_DIM, S).transpose(0, 3, 1, 2), v_p.reshape(B, S, N_HEADS, V_DIM),
            k_s.reshape(Bd, Sd, 2 * N_HEADS, HEAD_DIM), v_s.reshape(Bd, Sd, N_HEADS, V_DIM),
            new_conv_p[None], new_h_p.reshape(1, B, D),
            tm(new_conv_s)[None], new_h_s[None])
```

```python
import functools
import math

import jax
import jax.numpy as jnp
from jax import lax
from jax.experimental import pallas as pl
from jax.experimental.pallas import tpu as pltpu

F32 = jnp.float32
BF16 = jnp.bfloat16

EPS = 1e-6
N_HEADS = 8
HEAD_DIM = 64
V_DIM = 2 * HEAD_DIM
N_RG_BLOCKS = 8
CONV_W = 4
RG_C = 8.0
ROPE_THETA = 10000.0
LANES = 128
SUBLANES = 8
NEG_BIG = -1e30
LOG2_E = math.log2(math.e)
ONES_ROWS = 16
VMEM_LIMIT = 56 * 1024 * 1024

TOKEN_TILE = 512
SEQ_TILE = 512
ATTN_CHUNK = 256
ATTN_JOINT = 4
ATTN_HEADS_PER_STEP = 2
PAGES_PER_STEP = 8


def _cparams(sem):
    return pltpu.CompilerParams(dimension_semantics=sem, vmem_limit_bytes=VMEM_LIMIT)


def _rms(x, g):
    return x * lax.rsqrt(jnp.mean(x * x, axis=-1, keepdims=True) + EPS) * g


def _rope(x, cos, sin_hi, sin_lo):
    outs = []
    for j in range(x.shape[1] // LANES):
        xj = x[:, j * LANES:(j + 1) * LANES]
        outs.append(xj * cos
                    + pltpu.roll(xj, LANES - HEAD_DIM // 2, 1) * sin_lo
                    + pltpu.roll(xj, HEAD_DIM // 2, 1) * sin_hi)
    return jnp.concatenate(outs, axis=1)


def _gelu_tanh(x):
    return 0.5 * x * (1.0 + jnp.tanh(math.sqrt(2.0 / math.pi) * (x + 0.044715 * (x * x * x))))


def _softplus(z):
    return jnp.maximum(z, 0.0) + jnp.log1p(jnp.exp(-jnp.abs(z)))


def _ffn_body(x_ref, g_ref, wgu_ref, wd_ref, o_ref, *, n_split):
    x = x_ref[...]
    n = _rms(x, g_ref[...]).astype(BF16)
    d_ff = wd_ref.shape[0]
    tf = d_ff // n_split
    y = None
    for c in range(n_split):
        g = jnp.dot(n, wgu_ref[:, c * tf:(c + 1) * tf], preferred_element_type=F32)
        u = jnp.dot(n, wgu_ref[:, d_ff + c * tf:d_ff + (c + 1) * tf], preferred_element_type=F32)
        h = (g * jax.nn.sigmoid(g) * u).astype(BF16)
        part = jnp.dot(h, wd_ref[c * tf:(c + 1) * tf, :], preferred_element_type=F32)
        y = part if y is None else y + part
    o_ref[...] = x + 0.5 * y


def _ffn_split(d_ff):
    for n_split in (2, 4, 1):
        if d_ff % (n_split * LANES) == 0:
            return n_split
    raise ValueError(f"unsupported FFN width {d_ff}")


def _ffn(x, g, w_gu_all, w_down_all, layer):
    T, D = x.shape
    d_ff = w_down_all.shape[1]
    tm = min(TOKEN_TILE, T)
    resident = pl.Buffered(1)
    return pl.pallas_call(
        functools.partial(_ffn_body, n_split=_ffn_split(d_ff)),
        out_shape=jax.ShapeDtypeStruct((T, D), F32),
        grid=(T // tm,),
        in_specs=[
            pl.BlockSpec((tm, D), lambda i: (i, 0)),
            pl.BlockSpec((1, D), lambda i: (0, 0)),
            pl.BlockSpec((None, D, 2 * d_ff), lambda i: (layer, 0, 0), pipeline_mode=resident),
            pl.BlockSpec((None, d_ff, D), lambda i: (layer, 0, 0), pipeline_mode=resident),
        ],
        out_specs=pl.BlockSpec((tm, D), lambda i: (i, 0)),
        compiler_params=_cparams(("parallel",)),
        name="ffn",
    )(x, g.reshape(1, D), w_gu_all, w_down_all)


def _win_body(x_ref, g_ref, w_ref, gate_ref, u_ref):
    n = _rms(x_ref[...], g_ref[...]).astype(BF16)
    y = jnp.dot(n, w_ref[...], preferred_element_type=F32)
    d = gate_ref.shape[1]
    gate_ref[...] = y[:, :d]
    u_ref[...] = y[:, d:]


def _win(x, g, w_in):
    T, D = x.shape
    N = w_in.shape[1] // 2
    tm = min(TOKEN_TILE, T)
    return pl.pallas_call(
        _win_body,
        out_shape=(jax.ShapeDtypeStruct((T, N), F32), jax.ShapeDtypeStruct((T, N), F32)),
        grid=(T // tm,),
        in_specs=[
            pl.BlockSpec((tm, D), lambda i: (i, 0)),
            pl.BlockSpec((1, D), lambda i: (0, 0)),
            pl.BlockSpec((D, 2 * N), lambda i: (0, 0)),
        ],
        out_specs=(pl.BlockSpec((tm, N), lambda i: (i, 0)), pl.BlockSpec((tm, N), lambda i: (i, 0))),
        compiler_params=_cparams(("parallel",)),
        name="rg_in_proj",
    )(x, g.reshape(1, D), w_in)


def _store_head_transposed(dst_ref, x):
    n_chunks, n_rows, chunk = dst_ref.shape[2], dst_ref.shape[3], dst_ref.shape[4]
    for h in range(N_HEADS):
        xt = x[:, h * V_DIM:(h + 1) * V_DIM].T.astype(dst_ref.dtype)
        for c in range(n_chunks):
            dst_ref[0, h, c, 0:V_DIM, :] = xt[:, c * chunk:(c + 1) * chunk]
            if n_rows > V_DIM:
                dst_ref[0, h, c, V_DIM:n_rows, :] = jnp.ones((n_rows - V_DIM, chunk), dst_ref.dtype)


def _head_transposed_out(B, S, tm, dtype, extra_rows=0):
    tiles_per_seq = S // tm
    rows = V_DIM + extra_rows
    shape = jax.ShapeDtypeStruct((B, N_HEADS, S // ATTN_CHUNK, rows, ATTN_CHUNK), dtype)
    spec = pl.BlockSpec((1, N_HEADS, tm // ATTN_CHUNK, rows, ATTN_CHUNK),
                        lambda i: (i // tiles_per_seq, 0, i % tiles_per_seq, 0, 0))
    return shape, spec


def _q_body(x_ref, g_ref, w_ref, cos_ref, shi_ref, slo_ref, q_ref, *, transposed):
    n = _rms(x_ref[...], g_ref[...]).astype(BF16)
    q = jnp.dot(n, w_ref[...], preferred_element_type=F32)
    q = _rope(q, cos_ref[...], shi_ref[...], slo_ref[...]) * (HEAD_DIM ** -0.5)
    if transposed:
        _store_head_transposed(q_ref, q * LOG2_E)
    else:
        q_ref[...] = q


def _table_specs(tm, n_tab_tiles):
    return [pl.BlockSpec((tm, LANES), lambda i: (i % n_tab_tiles, 0))] * 3


def _q_proj(x, g, w_q, tabs, seq_shape=None):
    T, D = x.shape
    N = w_q.shape[1]
    tm = min(TOKEN_TILE, T)
    if seq_shape is None:
        out_shape, out_spec = jax.ShapeDtypeStruct((T, N), F32), pl.BlockSpec((tm, N), lambda i: (i, 0))
    else:
        out_shape, out_spec = _head_transposed_out(*seq_shape, tm, BF16)
    return pl.pallas_call(
        functools.partial(_q_body, transposed=seq_shape is not None),
        out_shape=out_shape,
        grid=(T // tm,),
        in_specs=[
            pl.BlockSpec((tm, D), lambda i: (i, 0)),
            pl.BlockSpec((1, D), lambda i: (0, 0)),
            pl.BlockSpec((D, N), lambda i: (0, 0)),
        ] + _table_specs(tm, tabs[0].shape[0] // tm),
        out_specs=out_spec,
        compiler_params=_cparams(("parallel",)),
        name="q_proj",
    )(x, g.reshape(1, D), w_q, *tabs)


def _kv_body(x_ref, g_ref, w_ref, cos_ref, shi_ref, slo_ref, k_ref, v_ref, *lowp_refs):
    n = _rms(x_ref[...], g_ref[...]).astype(BF16)
    kv = jnp.dot(n, w_ref[...], preferred_element_type=F32)
    kd = N_HEADS * V_DIM
    k = _rope(kv[:, :kd], cos_ref[...], shi_ref[...], slo_ref[...])
    v = kv[:, kd:]
    if not lowp_refs:
        k_ref[...] = k
        v_ref[...] = v
    else:
        for h in range(N_HEADS):
            k_ref[0, h] = k[:, h * V_DIM:(h + 1) * V_DIM].T
        v_ref[...] = v
        kb_ref, vt_ref = lowp_refs
        kb_ref[...] = k.astype(kb_ref.dtype)
        _store_head_transposed(vt_ref, v)


def _kv_proj(x, g, w_kv, tabs, kd, seq_shape=None):
    T, D = x.shape
    N = w_kv.shape[1]
    vd = N - kd
    tm = min(TOKEN_TILE, T)
    out_shape = [jax.ShapeDtypeStruct((T, kd), F32), jax.ShapeDtypeStruct((T, vd), F32)]
    out_specs = [pl.BlockSpec((tm, kd), lambda i: (i, 0)), pl.BlockSpec((tm, vd), lambda i: (i, 0))]
    if seq_shape is not None:
        B, S = seq_shape
        tiles_per_seq = S // tm
        out_shape[0] = jax.ShapeDtypeStruct((B, N_HEADS, V_DIM, S), F32)
        out_specs[0] = pl.BlockSpec((1, N_HEADS, V_DIM, tm), lambda i: (i // tiles_per_seq, 0, 0, i % tiles_per_seq))
        vt_shape, vt_spec = _head_transposed_out(B, S, tm, BF16, extra_rows=ONES_ROWS)
        out_shape += [jax.ShapeDtypeStruct((T, kd), BF16), vt_shape]
        out_specs += [pl.BlockSpec((tm, kd), lambda i: (i, 0)), vt_spec]
    return pl.pallas_call(
        _kv_body,
        out_shape=tuple(out_shape),
        grid=(T // tm,),
        in_specs=[
            pl.BlockSpec((tm, D), lambda i: (i, 0)),
            pl.BlockSpec((1, D), lambda i: (0, 0)),
            pl.BlockSpec((D, N), lambda i: (0, 0)),
        ] + _table_specs(tm, tabs[0].shape[0] // tm),
        out_specs=tuple(out_specs),
        compiler_params=_cparams(("parallel",)),
        name="kv_proj",
    )(x, g.reshape(1, D), w_kv, *tabs)


def _ple_body(x_ref, p_ref, g_ref, wg_ref, wp_ref, fin_ref, o_ref, *, final):
    x = x_ref[...]
    n = _rms(x, g_ref[...]).astype(BF16)
    gate = jax.nn.sigmoid(jnp.dot(n, wg_ref[...], preferred_element_type=F32))
    proj = jnp.dot(p_ref[...].astype(BF16), wp_ref[...], preferred_element_type=F32)
    y = x + gate * proj
    if final:
        y = _rms(y, fin_ref[...])
    o_ref[...] = y


def _ple(x, p_all, layer, g, w_gate, w_proj, fin_g, final):
    T, D = x.shape
    P = p_all.shape[2]
    tm = min(TOKEN_TILE, T)
    return pl.pallas_call(
        functools.partial(_ple_body, final=final),
        out_shape=jax.ShapeDtypeStruct((T, D), F32),
        grid=(T // tm,),
        in_specs=[
            pl.BlockSpec((tm, D), lambda i: (i, 0)),
            pl.BlockSpec((None, tm, P), lambda i: (layer, i, 0)),
            pl.BlockSpec((1, D), lambda i: (0, 0)),
            pl.BlockSpec((D, D), lambda i: (0, 0)),
            pl.BlockSpec((P, D), lambda i: (0, 0)),
            pl.BlockSpec((1, D), lambda i: (0, 0)),
        ],
        out_specs=pl.BlockSpec((tm, D), lambda i: (i, 0)),
        compiler_params=_cparams(("parallel",)),
        name="ple_final" if final else "ple",
    )(x, p_all, g.reshape(1, D), w_gate, w_proj, fin_g.reshape(1, D))


def _proj_res_body(a_ref, w_ref, r_ref, o_ref):
    o_ref[...] = r_ref[...] + jnp.dot(a_ref[...].astype(BF16), w_ref[...], preferred_element_type=F32)


def _proj_residual(a, w, res):
    T, K = a.shape
    N = w.shape[1]
    tm = min(TOKEN_TILE, T)
    return pl.pallas_call(
        _proj_res_body,
        out_shape=jax.ShapeDtypeStruct((T, N), F32),
        grid=(T // tm,),
        in_specs=[
            pl.BlockSpec((tm, K), lambda i: (i, 0)),
            pl.BlockSpec((K, N), lambda i: (0, 0)),
            pl.BlockSpec((tm, N), lambda i: (i, 0)),
        ],
        out_specs=pl.BlockSpec((tm, N), lambda i: (i, 0)),
        compiler_params=_cparams(("parallel",)),
        name="attn_out_proj",
    )(a, w, res)


def _rg_coeffs(conv, wai_ref, ba, bi, sp):
    cb = conv.astype(BF16)
    nb = wai_ref.shape[0]
    blk = conv.shape[1] // nb
    a_parts, b_parts = [], []
    for n in range(nb):
        sl = slice(n * blk, (n + 1) * blk)
        g = jnp.dot(cb[:, sl], wai_ref[n], preferred_element_type=F32)
        r = jax.nn.sigmoid(g[:, :blk] + ba[:, sl])
        i = jax.nn.sigmoid(g[:, blk:] + bi[:, sl])
        log_a = -RG_C * r * sp[:, sl]
        a = jnp.exp(log_a)
        mult = jnp.sqrt(-jnp.tanh(log_a) * (a * a + 1.0))
        a_parts.append(a)
        b_parts.append(mult * (i * conv[:, sl]))
    return a_parts, b_parts


def _rg_prompt_body(gate_ref, u_ref, x_ref, cw_ref, cb_ref, wai_ref, ba_ref, bi_ref, lam_ref, wo_ref,
                    o_ref, nconv_ref, nh_ref, ubuf_ref, a_ref, h_ref, hstate_ref):
    s = pl.program_id(1)
    ts, D = u_ref.shape[1], u_ref.shape[2]

    @pl.when(s == 0)
    def _():
        ubuf_ref[0:SUBLANES, :] = jnp.zeros((SUBLANES, D), F32)
        hstate_ref[...] = jnp.zeros_like(hstate_ref)

    u = u_ref[0]
    ubuf_ref[SUBLANES:SUBLANES + ts, :] = u
    cw = cw_ref[...]
    conv = cb_ref[...] + u * cw[CONV_W - 1:CONV_W, :]
    for k in range(1, CONV_W):
        conv = conv + ubuf_ref[SUBLANES - k:SUBLANES - k + ts, :] * cw[CONV_W - 1 - k:CONV_W - k, :]

    sp = _softplus(-lam_ref[...])
    a, b = _rg_coeffs(conv, wai_ref, ba_ref[...], bi_ref[...], sp)
    a_ref[...] = jnp.concatenate(a, axis=1)
    h_ref[...] = jnp.concatenate(b, axis=1)

    row = lax.broadcasted_iota(jnp.int32, (SUBLANES, D), 0)

    def group(gi, h_prev):
        r0 = pl.multiple_of(gi * SUBLANES, SUBLANES)
        ag = a_ref[pl.ds(r0, SUBLANES), :]
        bg = h_ref[pl.ds(r0, SUBLANES), :]
        for d in (1, 2, 4):
            m = row >= d
            a_sh = pltpu.roll(ag, d, 0)
            b_sh = pltpu.roll(bg, d, 0)
            bg = jnp.where(m, ag * b_sh + bg, bg)
            ag = jnp.where(m, ag * a_sh, ag)
        hg = ag * h_prev + bg
        h_ref[pl.ds(r0, SUBLANES), :] = hg
        return jnp.broadcast_to(hg[SUBLANES - 1:SUBLANES, :], (SUBLANES, D))

    h_last = lax.fori_loop(0, ts // SUBLANES, group, hstate_ref[...])
    hstate_ref[...] = h_last
    ubuf_ref[0:SUBLANES, :] = u[ts - SUBLANES:ts, :]

    y = (_gelu_tanh(gate_ref[0]) * h_ref[...]).astype(BF16)
    o_ref[0] = x_ref[0] + jnp.dot(y, wo_ref[...], preferred_element_type=F32)

    @pl.when(s == pl.num_programs(1) - 1)
    def _():
        nconv_ref[0] = u[ts - (CONV_W - 1):ts, :]
        nh_ref[0] = h_last[0:1, :]


def _rg_prompt(gate, u, x, cw, cb, wai, ba, bi, lam, wo):
    B, S, D = u.shape
    ts = min(SEQ_TILE, S)
    row = lambda a: a.reshape(1, D)
    tok = pl.BlockSpec((1, ts, D), lambda b, s: (b, s, 0))
    vec = pl.BlockSpec((1, D), lambda b, s: (0, 0))
    return pl.pallas_call(
        _rg_prompt_body,
        out_shape=(jax.ShapeDtypeStruct((B, S, D), F32),
                   jax.ShapeDtypeStruct((B, CONV_W - 1, D), F32),
                   jax.ShapeDtypeStruct((B, 1, D), F32)),
        grid=(B, S // ts),
        in_specs=[tok, tok, tok,
                  pl.BlockSpec((CONV_W, D), lambda b, s: (0, 0)), vec,
                  pl.BlockSpec(wai.shape, lambda b, s: (0, 0, 0)), vec, vec, vec,
                  pl.BlockSpec((D, D), lambda b, s: (0, 0))],
        out_specs=(tok,
                   pl.BlockSpec((1, CONV_W - 1, D), lambda b, s: (b, 0, 0)),
                   pl.BlockSpec((1, 1, D), lambda b, s: (b, 0, 0))),
        scratch_shapes=[pltpu.VMEM((ts + SUBLANES, D), F32), pltpu.VMEM((ts, D), F32),
                        pltpu.VMEM((ts, D), F32), pltpu.VMEM((SUBLANES, D), F32)],
        compiler_params=_cparams(("parallel", "arbitrary")),
        name="rg_prompt",
    )(gate, u, x, cw, row(cb), wai, row(ba), row(bi), row(lam), wo)


def _rg_sample_body(gate_ref, u_ref, x_ref, cprev_ref, h0_ref, cw_ref, cb_ref, wai_ref, ba_ref, bi_ref,
                    lam_ref, wo_ref, o_ref, nconv_ref, nh_ref):
    n_t = u_ref.shape[0]
    cw = cw_ref[...]
    hist = [cprev_ref[k] for k in range(CONV_W - 1)] + [u_ref[t] for t in range(n_t)]
    sp = _softplus(-lam_ref[...])
    h = h0_ref[...]
    for t in range(n_t):
        conv = cb_ref[...] + hist[t] * cw[0:1, :]
        for k in range(1, CONV_W):
            conv = conv + hist[t + k] * cw[k:k + 1, :]
        a, b = _rg_coeffs(conv, wai_ref, ba_ref[...], bi_ref[...], sp)
        h = jnp.concatenate(a, axis=1) * h + jnp.concatenate(b, axis=1)
        y = (_gelu_tanh(gate_ref[t]) * h).astype(BF16)
        o_ref[t] = x_ref[t] + jnp.dot(y, wo_ref[...], preferred_element_type=F32)
    for k in range(CONV_W - 1):
        nconv_ref[k] = hist[n_t + k]
    nh_ref[...] = h


def _rg_sample(gate, u, x, cprev, h0, cw, cb, wai, ba, bi, lam, wo):
    n_t, Bd, D = u.shape
    row = lambda a: a.reshape(1, D)
    return pl.pallas_call(
        _rg_sample_body,
        out_shape=(jax.ShapeDtypeStruct((n_t, Bd, D), F32),
                   jax.ShapeDtypeStruct((CONV_W - 1, Bd, D), F32),
                   jax.ShapeDtypeStruct((Bd, D), F32)),
        compiler_params=pltpu.CompilerParams(vmem_limit_bytes=VMEM_LIMIT),
        name="rg_sample",
    )(gate, u, x, cprev, h0, cw, row(cb), wai, row(ba), row(bi), row(lam), wo)


def _diff_lambda(lq1, lk1, lq2, lk2, lambda_init):
    return (jnp.exp(jnp.sum(lq1 * lk1, axis=-1, keepdims=True))
            - jnp.exp(jnp.sum(lq2 * lk2, axis=-1, keepdims=True)) + lambda_init)


def _diff_norm(o0, o1, lam, subln, lambda_init):
    d = o0 - lam * o1
    return _rms(d, subln) * (1.0 - lambda_init)


def _flash_body(qt_ref, k_ref, vt_ref, lq1_ref, lk1_ref, lq2_ref, lk2_ref, sub_ref, o_ref, w_ref, acc_ref,
                m_ref, *, lambda_init):
    i = pl.program_id(2)
    n_heads = qt_ref.shape[1]
    tq = qt_ref.shape[4]
    chunk = vt_ref.shape[4]

    for hh in range(n_heads):
        qt = qt_ref[0, hh, 0]
        feat = lax.broadcasted_iota(jnp.int32, qt.shape, 0)
        zero = jnp.zeros_like(qt)
        w_ref[hh, :, 0:tq] = jnp.where(feat < HEAD_DIM, qt, zero)
        w_ref[hh, :, tq:2 * tq] = jnp.where(feat < HEAD_DIM, zero, qt)
    acc_ref[...] = jnp.zeros_like(acc_ref)
    m_ref[...] = jnp.full_like(m_ref, NEG_BIG)

    def attend(hh, first, n_full, diagonal):
        cs = [first + t for t in range(n_full)] + ([i] if diagonal else [])
        s = []
        for t, c in enumerate(cs):
            r0 = pl.multiple_of(c * chunk, chunk)
            st = jnp.dot(k_ref[0, pl.ds(r0, chunk), V_DIM * hh:V_DIM * (hh + 1)], w_ref[hh],
                         preferred_element_type=F32)
            if diagonal and t == n_full:
                key = lax.broadcasted_iota(jnp.int32, st.shape, 0)
                qry = lax.broadcasted_iota(jnp.int32, st.shape, 1)
                qry = jnp.where(qry >= tq, qry - tq, qry)
                st = jnp.where(key <= qry, st, NEG_BIG)
            s.append(st)
        m_prev = m_ref[hh]
        m_new = m_prev
        for st in s:
            m_new = jnp.maximum(m_new, jnp.max(st, axis=0, keepdims=True))
        alpha = jnp.exp2(m_prev - m_new)
        pv = None
        for st, c in zip(s, cs):
            p = jnp.exp2(st - m_new).astype(BF16)
            part = jnp.dot(vt_ref[0, hh, c], p, preferred_element_type=F32)
            pv = part if pv is None else pv + part
        acc_ref[hh] = alpha * acc_ref[hh] + pv
        m_ref[hh] = m_new

    def main(g, carry):
        for hh in range(n_heads):
            attend(hh, g * ATTN_JOINT, ATTN_JOINT, False)
        return carry

    lax.fori_loop(0, i // ATTN_JOINT, main, 0)
    for r in range(ATTN_JOINT):
        @pl.when(i % ATTN_JOINT == r)
        def _():
            for hh in range(n_heads):
                attend(hh, i - r, r, True)

    lam = _diff_lambda(lq1_ref[...], lk1_ref[...], lq2_ref[...], lk2_ref[...], lambda_init)
    for hh in range(n_heads):
        acc = acc_ref[hh]
        o = acc[0:V_DIM] / acc[V_DIM:V_DIM + 1]
        d = (o[:, 0:tq] - lam * o[:, tq:2 * tq]).T
        o_ref[0, :, V_DIM * hh:V_DIM * (hh + 1)] = (_rms(d, sub_ref[...]) * (1.0 - lambda_init)).astype(o_ref.dtype)


def _flash_prompt(qt, kb, vt, lq1, lk1, lq2, lk2, subln, lambda_init):
    B, S, W = kb.shape
    n_chunks, chunk = qt.shape[2], qt.shape[4]
    hp = ATTN_HEADS_PER_STEP
    vec = lambda n: pl.BlockSpec((1, n), lambda b, h, i: (0, 0))
    row = lambda a: a.reshape(1, -1)
    return pl.pallas_call(
        functools.partial(_flash_body, lambda_init=lambda_init),
        out_shape=jax.ShapeDtypeStruct((B, S, W), BF16),
        grid=(B, N_HEADS // hp, n_chunks),
        in_specs=[
            pl.BlockSpec((1, hp, 1, V_DIM, chunk), lambda b, h, i: (b, h, i, 0, 0)),
            pl.BlockSpec((1, S, hp * V_DIM), lambda b, h, i: (b, 0, h)),
            pl.BlockSpec((1, hp, n_chunks, vt.shape[3], chunk), lambda b, h, i: (b, h, 0, 0, 0)),
            vec(HEAD_DIM), vec(HEAD_DIM), vec(HEAD_DIM), vec(HEAD_DIM), vec(V_DIM),
        ],
        out_specs=pl.BlockSpec((1, chunk, hp * V_DIM), lambda b, h, i: (b, i, h)),
        scratch_shapes=[pltpu.VMEM((hp, V_DIM, 2 * chunk), BF16), pltpu.VMEM((hp, vt.shape[3], 2 * chunk), F32),
                        pltpu.VMEM((hp, 1, 2 * chunk), F32)],
        compiler_params=_cparams(("parallel", "parallel", "arbitrary")),
        name="flash_diff_attn",
    )(qt, kb, vt, row(lq1), row(lk1), row(lq2), row(lk2), row(subln))


def _paged_body(pt_ref, q_ref, kn_ref, vn_ref, *rest, n_pp, lambda_init):
    kt_refs = rest[:n_pp]
    v_refs = rest[n_pp:2 * n_pp]
    lq1_ref, lk1_ref, lq2_ref, lk2_ref, sub_ref, o_ref, qblk_ref, m_ref, l_ref, acc_ref = rest[2 * n_pp:]
    j = pl.program_id(1)
    n_q = q_ref.shape[1]
    page = kt_refs[0].shape[2]
    grp = 2 * n_q

    def update(s, head_pv):
        m_prev = m_ref[...]
        m_new = jnp.maximum(m_prev, jnp.max(s, axis=1, keepdims=True))
        alpha = jnp.exp(m_prev - m_new)
        p = jnp.exp(s - m_new)
        l_ref[...] = alpha * l_ref[...] + jnp.sum(p, axis=1, keepdims=True)
        m_ref[...] = m_new
        for h in range(N_HEADS):
            sl = slice(grp * h, grp * (h + 1))
            acc_ref[sl, :] = alpha[sl] * acc_ref[sl, :] + head_pv(p[sl], h)

    @pl.when(j == 0)
    def _():
        q = q_ref[0]
        r = lax.broadcasted_iota(jnp.int32, qblk_ref.shape, 0)
        c = lax.broadcasted_iota(jnp.int32, qblk_ref.shape, 1)
        qrep = jnp.zeros(qblk_ref.shape, F32)
        for t in range(n_q):
            qrep = jnp.where(r % n_q == t, jnp.broadcast_to(q[t:t + 1, :], qblk_ref.shape), qrep)
        qblk_ref[...] = jnp.where(c // HEAD_DIM == r // n_q, qrep, 0.0)
        m_ref[...] = jnp.full_like(m_ref, NEG_BIG)
        l_ref[...] = jnp.zeros_like(l_ref)
        acc_ref[...] = jnp.zeros_like(acc_ref)
        kn, vn = kn_ref[0], vn_ref[0]
        s = lax.dot_general(qblk_ref[...], kn, (((1,), (1,)), ((), ())), preferred_element_type=F32)
        rn = lax.broadcasted_iota(jnp.int32, s.shape, 0)
        cn = lax.broadcasted_iota(jnp.int32, s.shape, 1)
        s = jnp.where(cn <= rn % n_q, s, NEG_BIG)
        update(s, lambda p, h: jnp.dot(p, vn[:, V_DIM * h:V_DIM * (h + 1)], preferred_element_type=F32))

    qblk = qblk_ref[...]
    s = jnp.concatenate([jnp.dot(qblk, kt_ref[0], preferred_element_type=F32) for kt_ref in kt_refs], axis=1)

    def head_pv(p, h):
        out = None
        for g, v_ref in enumerate(v_refs):
            part = jnp.dot(p[:, page * g:page * (g + 1)], v_ref[0, pl.ds(h, page, stride=N_HEADS), :],
                           preferred_element_type=F32)
            out = part if out is None else out + part
        return out

    update(s, head_pv)

    @pl.when(j == pl.num_programs(1) - 1)
    def _():
        o = acc_ref[...] / l_ref[...]
        lam = _diff_lambda(lq1_ref[...], lk1_ref[...], lq2_ref[...], lk2_ref[...], lambda_init)
        for h in range(N_HEADS):
            blk = o[grp * h:grp * (h + 1)]
            o_ref[0, n_q * h:n_q * (h + 1), :] = _diff_norm(blk[0:n_q], blk[n_q:grp], lam, sub_ref[...], lambda_init)


def _paged_attn(q, k_new, v_new, cache_k, cache_v, page_table, lq1, lk1, lq2, lk2, subln, lambda_init):
    Bd, n_q, W = q.shape
    n_pool, page = cache_k.shape[0], cache_k.shape[1]
    n_pages = page_table.shape[1]
    n_pp = next(n for n in (PAGES_PER_STEP, 4, 2, 1) if n_pages % n == 0)
    n_steps = n_pages // n_pp
    rows = 2 * N_HEADS * n_q
    ckt = cache_k.transpose(0, 2, 3, 1).reshape(n_pool, W, page)
    cv = cache_v.reshape(n_pool, page * N_HEADS, V_DIM)
    pt = page_table.reshape(-1)

    def page_spec(shape, i):
        return pl.BlockSpec((1,) + shape[1:], lambda b, j, pt: (pt[b * n_pages + j * n_pp + i], 0, 0))

    per_seq = lambda a: pl.BlockSpec((1,) + a.shape[1:], lambda b, j, pt: (b, 0, 0))
    vec = lambda n: pl.BlockSpec((1, n), lambda b, j, pt: (0, 0))
    row = lambda a: a.reshape(1, -1)
    return pl.pallas_call(
        functools.partial(_paged_body, n_pp=n_pp, lambda_init=lambda_init),
        out_shape=jax.ShapeDtypeStruct((Bd, N_HEADS * n_q, V_DIM), F32),
        grid_spec=pltpu.PrefetchScalarGridSpec(
            num_scalar_prefetch=1,
            grid=(Bd, n_steps),
            in_specs=[per_seq(q), per_seq(k_new), per_seq(v_new)]
                     + [page_spec(ckt.shape, i) for i in range(n_pp)]
                     + [page_spec(cv.shape, i) for i in range(n_pp)]
                     + [vec(HEAD_DIM)] * 4 + [vec(V_DIM)],
            out_specs=pl.BlockSpec((1, N_HEADS * n_q, V_DIM), lambda b, j, pt: (b, 0, 0)),
            scratch_shapes=[pltpu.VMEM((rows, W), F32), pltpu.VMEM((rows, 1), F32),
                            pltpu.VMEM((rows, 1), F32), pltpu.VMEM((rows, V_DIM), F32)],
        ),
        compiler_params=_cparams(("parallel", "arbitrary")),
        name="paged_diff_attn",
    )(pt, q, k_new, v_new, *([ckt] * n_pp), *([cv] * n_pp), row(lq1), row(lk1), row(lq2), row(lk2), row(subln))


def _rope_tables(pos):
    half = HEAD_DIM // 2
    lane = jnp.arange(LANES)
    inv = jnp.power(ROPE_THETA, -(lane % half).astype(F32) * 2.0 / HEAD_DIM)
    ang = pos.astype(F32)[:, None] * inv[None, :]
    cos, sin = jnp.cos(ang), jnp.sin(ang)
    upper = (lane % HEAD_DIM) >= half
    return cos, jnp.where(upper, sin, 0.0), jnp.where(upper, 0.0, -sin)


def kernel(x_prompt, x_sample, p_prompt, p_sample, cache_k, cache_v, page_table, state_conv, state_rglru, ffn1_norm, ffn1_w_gu, ffn1_w_down, mix_norm, rg_w_in, rg_conv_w, rg_conv_b, rg_w_a, rg_b_a, rg_w_i, rg_b_i, rg_lambda, rg_w_out, kv_norm, w_kv, attn_w_q, lambda_q1, lambda_k1, lambda_q2, lambda_k2, attn_subln, attn_w_o, ffn2_norm, ffn2_w_gu, ffn2_w_down, ple_norm, ple_w_gate, ple_w_proj, final_norm):
    B, S, D = x_prompt.shape
    Bd, Sd, _ = x_sample.shape
    depth = ffn1_norm.shape[0]
    n_a = rg_w_in.shape[0]
    assert depth == 2 and n_a == 1, "one recurrent layer followed by one attention layer"
    assert D == N_HEADS * V_DIM
    n_pages, page = page_table.shape[1], cache_k.shape[1]
    past_len = n_pages * page
    kd = 2 * N_HEADS * HEAD_DIM
    lambda_init = 0.8 - 0.6 * math.exp(-0.3 * 1)

    bf = lambda w: w.astype(BF16)
    w_gu1, w_dn1, w_gu2, w_dn2 = bf(ffn1_w_gu), bf(ffn1_w_down), bf(ffn2_w_gu), bf(ffn2_w_down)
    w_in, w_out, w_kvb, w_q, w_o = bf(rg_w_in[0]), bf(rg_w_out[0]), bf(w_kv), bf(attn_w_q[0]), bf(attn_w_o[0])
    w_pg, w_pp = bf(ple_w_gate), bf(ple_w_proj)
    wai = bf(jnp.concatenate([rg_w_a[0], rg_w_i[0]], axis=-1))

    def layer0_pre(x):
        x = _ffn(x, ffn1_norm[0], w_gu1, w_dn1, 0)
        gate, u = _win(x, mix_norm[0], w_in)
        return x, gate, u

    def layer0_post(x, p_all, tabs, seq_shape):
        x = _ffn(x, ffn2_norm[0], w_gu2, w_dn2, 0)
        x = _ple(x, p_all, 0, ple_norm[0], w_pg[0], w_pp[0], final_norm, False)
        kv = _kv_proj(x, kv_norm, w_kvb, tabs, kd, seq_shape)
        x = _ffn(x, ffn1_norm[1], w_gu1, w_dn1, 1)
        q = _q_proj(x, mix_norm[1], w_q, tabs, seq_shape)
        return x, q, kv

    def layer1_post(x, dn, p_all):
        x = _proj_residual(dn, w_o, x)
        x = _ffn(x, ffn2_norm[1], w_gu2, w_dn2, 1)
        return _ple(x, p_all, 1, ple_norm[1], w_pg[1], w_pp[1], final_norm, True)

    rg_args = (rg_conv_w[0], rg_conv_b[0], wai, rg_b_a[0], rg_b_i[0], rg_lambda[0], w_out)
    lam_args = (lambda_q1[0], lambda_k1[0], lambda_q2[0], lambda_k2[0], attn_subln[0], lambda_init)

    Tp = B * S
    tabs_p = _rope_tables(jnp.arange(S))
    p_p = p_prompt.reshape(depth, Tp, -1)
    x, gate, u = layer0_pre(x_prompt.reshape(Tp, D))
    x, new_conv_p, new_h_p = _rg_prompt(gate.reshape(B, S, D), u.reshape(B, S, D), x.reshape(B, S, D), *rg_args)
    x, qt, (k_p, v_p, kb, vt) = layer0_post(x.reshape(Tp, D), p_p, tabs_p, (B, S))
    dn = _flash_prompt(qt, kb.reshape(B, S, D), vt, *lam_args)
    y_prompt = layer1_post(x, dn.reshape(Tp, D), p_p).reshape(B, S, D)

    Ts = Sd * Bd
    tm = lambda a: jnp.swapaxes(a, 0, 1)
    tabs_s = _rope_tables(jnp.repeat(past_len + jnp.arange(Sd), Bd))
    p_s = jnp.swapaxes(p_sample, 1, 2).reshape(depth, Ts, -1)
    x, gate, u = layer0_pre(tm(x_sample).reshape(Ts, D))
    x, new_conv_s, new_h_s = _rg_sample(gate.reshape(Sd, Bd, D), u.reshape(Sd, Bd, D), x.reshape(Sd, Bd, D),
                                        tm(state_conv[0]), state_rglru[0], *rg_args)
    x, q, (k_s, v_s) = layer0_post(x.reshape(Ts, D), p_s, tabs_s, None)
    bm = lambda a: tm(a.reshape(Sd, Bd, -1))
    k_s, v_s = bm(k_s), bm(v_s)
    pad = lambda a: jnp.pad(a, ((0, 0), (0, -Sd % SUBLANES), (0, 0)))
    dn = _paged_attn(bm(q), pad(k_s), pad(v_s), cache_k, cache_v, page_table, *lam_args)
    dn = dn.reshape(Bd, N_HEADS, Sd, V_DIM).transpose(2, 0, 1, 3).reshape(Ts, D)
    y_sample = bm(layer1_post(x, dn, p_s))

    return (y_prompt, y_sample,
            k_p.reshape(B, 2 * N_HEADS, HEAD_DIM, S).transpose(0, 3, 1, 2), v_p.reshape(B, S, N_HEADS, V_DIM),
            k_s.reshape(Bd, Sd, 2 * N_HEADS, HEAD_DIM), v_s.reshape(Bd, Sd, N_HEADS, V_DIM),
            new_conv_p[None], new_h_p.reshape(1, B, D),
            tm(new_conv_s)[None], new_h_s[None])
```

```python
import functools
import math

import jax
import jax.numpy as jnp
from jax import lax
from jax.experimental import pallas as pl
from jax.experimental.pallas import tpu as pltpu

F32 = jnp.float32
BF16 = jnp.bfloat16

EPS = 1e-6
N_HEADS = 8
HEAD_DIM = 64
V_DIM = 2 * HEAD_DIM
N_RG_BLOCKS = 8
CONV_W = 4
RG_C = 8.0
ROPE_THETA = 10000.0
LANES = 128
SUBLANES = 8
NEG_BIG = -1e30
LOG2_E = math.log2(math.e)
ONES_ROWS = 16
VMEM_LIMIT = 56 * 1024 * 1024

TOKEN_TILE = 512
SEQ_TILE = 512
ATTN_CHUNK = 256
ATTN_JOINT = 4
ATTN_HEADS_PER_STEP = 4
PAGES_PER_STEP = 8


def _cparams(sem):
    return pltpu.CompilerParams(dimension_semantics=sem, vmem_limit_bytes=VMEM_LIMIT)


def _rms(x, g):
    return x * lax.rsqrt(jnp.mean(x * x, axis=-1, keepdims=True) + EPS) * g


def _rope(x, cos, sin_hi, sin_lo):
    outs = []
    for j in range(x.shape[1] // LANES):
        xj = x[:, j * LANES:(j + 1) * LANES]
        outs.append(xj * cos
                    + pltpu.roll(xj, LANES - HEAD_DIM // 2, 1) * sin_lo
                    + pltpu.roll(xj, HEAD_DIM // 2, 1) * sin_hi)
    return jnp.concatenate(outs, axis=1)


def _gelu_tanh(x):
    return 0.5 * x * (1.0 + jnp.tanh(math.sqrt(2.0 / math.pi) * (x + 0.044715 * (x * x * x))))


def _softplus(z):
    return jnp.maximum(z, 0.0) + jnp.log1p(jnp.exp(-jnp.abs(z)))


def _ffn_split(d_ff):
    for n_split in (2, 4, 1):
        if d_ff % (n_split * LANES) == 0:
            return n_split
    raise ValueError(f"unsupported FFN width {d_ff}")


def _store_head_transposed(dst_ref, x):
    n_chunks, n_rows, chunk = dst_ref.shape[2], dst_ref.shape[3], dst_ref.shape[4]
    for h in range(N_HEADS):
        xt = x[:, h * V_DIM:(h + 1) * V_DIM].T.astype(dst_ref.dtype)
        for c in range(n_chunks):
            dst_ref[0, h, c, 0:V_DIM, :] = xt[:, c * chunk:(c + 1) * chunk]
            if n_rows > V_DIM:
                dst_ref[0, h, c, V_DIM:n_rows, :] = jnp.ones((n_rows - V_DIM, chunk), dst_ref.dtype)


def _head_transposed_out(B, S, tm, dtype, extra_rows=0):
    tiles_per_seq = S // tm
    rows = V_DIM + extra_rows
    shape = jax.ShapeDtypeStruct((B, N_HEADS, S // ATTN_CHUNK, rows, ATTN_CHUNK), dtype)
    spec = pl.BlockSpec((1, N_HEADS, tm // ATTN_CHUNK, rows, ATTN_CHUNK),
                        lambda i: (i // tiles_per_seq, 0, i % tiles_per_seq, 0, 0))
    return shape, spec


def _table_specs(tm, n_tab_tiles):
    return [pl.BlockSpec((tm, LANES), lambda i: (i % n_tab_tiles, 0))] * 3


def _kv_body(x_ref, g_ref, w_ref, cos_ref, shi_ref, slo_ref, k_ref, v_ref, *lowp_refs):
    n = _rms(x_ref[...], g_ref[...]).astype(BF16)
    kv = jnp.dot(n, w_ref[...], preferred_element_type=F32)
    kd = N_HEADS * V_DIM
    k = _rope(kv[:, :kd], cos_ref[...], shi_ref[...], slo_ref[...])
    v = kv[:, kd:]
    if not lowp_refs:
        k_ref[...] = k
        v_ref[...] = v
    else:
        for h in range(N_HEADS):
            k_ref[0, h] = k[:, h * V_DIM:(h + 1) * V_DIM].T
        v_ref[...] = v
        kb_ref, vt_ref = lowp_refs
        kb_ref[...] = k.astype(kb_ref.dtype)
        _store_head_transposed(vt_ref, v)


def _kv_proj(x, g, w_kv, tabs, kd, seq_shape=None):
    T, D = x.shape
    N = w_kv.shape[1]
    vd = N - kd
    tm = min(TOKEN_TILE, T)
    out_shape = [jax.ShapeDtypeStruct((T, kd), F32), jax.ShapeDtypeStruct((T, vd), F32)]
    out_specs = [pl.BlockSpec((tm, kd), lambda i: (i, 0)), pl.BlockSpec((tm, vd), lambda i: (i, 0))]
    if seq_shape is not None:
        B, S = seq_shape
        tiles_per_seq = S // tm
        out_shape[0] = jax.ShapeDtypeStruct((B, N_HEADS, V_DIM, S), F32)
        out_specs[0] = pl.BlockSpec((1, N_HEADS, V_DIM, tm), lambda i: (i // tiles_per_seq, 0, 0, i % tiles_per_seq))
        vt_shape, vt_spec = _head_transposed_out(B, S, tm, BF16, extra_rows=ONES_ROWS)
        out_shape += [jax.ShapeDtypeStruct((T, kd), BF16), vt_shape]
        out_specs += [pl.BlockSpec((tm, kd), lambda i: (i, 0)), vt_spec]
    return pl.pallas_call(
        _kv_body,
        out_shape=tuple(out_shape),
        grid=(T // tm,),
        in_specs=[
            pl.BlockSpec((tm, D), lambda i: (i, 0)),
            pl.BlockSpec((1, D), lambda i: (0, 0)),
            pl.BlockSpec((D, N), lambda i: (0, 0)),
        ] + _table_specs(tm, tabs[0].shape[0] // tm),
        out_specs=tuple(out_specs),
        compiler_params=_cparams(("parallel",)),
        name="kv_proj",
    )(x, g.reshape(1, D), w_kv, *tabs)


def _block_body(*refs, pre, post, n_split, transposed_q, final):
    refs = list(refs)
    take = lambda k: [refs.pop(0) for _ in range(k)]
    (x_ref,) = take(1)
    x = x_ref[...]
    if pre == "attn_out":
        dn_ref, wo_ref = take(2)
        x = x + jnp.dot(dn_ref[...].astype(BF16), wo_ref[...], preferred_element_type=F32)
    g_ref, wgu_ref, wd_ref = take(3)

    n = _rms(x, g_ref[...]).astype(BF16)
    d_ff = wd_ref.shape[0]
    tf = d_ff // n_split
    y = None
    for c in range(n_split):
        g = jnp.dot(n, wgu_ref[:, c * tf:(c + 1) * tf], preferred_element_type=F32)
        u = jnp.dot(n, wgu_ref[:, d_ff + c * tf:d_ff + (c + 1) * tf], preferred_element_type=F32)
        h = (g * jax.nn.sigmoid(g) * u).astype(BF16)
        part = jnp.dot(h, wd_ref[c * tf:(c + 1) * tf, :], preferred_element_type=F32)
        y = part if y is None else y + part
    x = x + 0.5 * y

    if post == "rg_in":
        mg_ref, win_ref, x_out, gate_ref, u_ref = take(5)
        x_out[...] = x
        z = jnp.dot(_rms(x, mg_ref[...]).astype(BF16), win_ref[...], preferred_element_type=F32)
        d = gate_ref.shape[1]
        gate_ref[...] = z[:, :d]
        u_ref[...] = z[:, d:]
    elif post == "q":
        mg_ref, wq_ref, cos_ref, shi_ref, slo_ref, x_out, q_ref = take(7)
        x_out[...] = x
        q = jnp.dot(_rms(x, mg_ref[...]).astype(BF16), wq_ref[...], preferred_element_type=F32)
        q = _rope(q, cos_ref[...], shi_ref[...], slo_ref[...]) * (HEAD_DIM ** -0.5)
        if transposed_q:
            _store_head_transposed(q_ref, q * LOG2_E)
        else:
            q_ref[...] = q
    else:
        p_ref, pg_ref, wpg_ref, wpp_ref, fin_ref, x_out = take(6)
        gate = jax.nn.sigmoid(jnp.dot(_rms(x, pg_ref[...]).astype(BF16), wpg_ref[...], preferred_element_type=F32))
        x = x + gate * jnp.dot(p_ref[...].astype(BF16), wpp_ref[...], preferred_element_type=F32)
        x_out[...] = _rms(x, fin_ref[...]) if final else x


def _ffn_block(x, g, w_gu_all, w_down_all, layer, post, *, attn_out=None, final=False):
    T, D = x.shape
    d_ff = w_down_all.shape[1]
    tm = min(TOKEN_TILE, T)
    resident = pl.Buffered(1)
    tok = lambda width: pl.BlockSpec((tm, width), lambda i: (i, 0))
    vec = pl.BlockSpec((1, D), lambda i: (0, 0))
    whole = lambda a: pl.BlockSpec(a.shape, lambda i: (0,) * a.ndim, pipeline_mode=resident)
    layer_of = lambda a: pl.BlockSpec((None,) + a.shape[1:], lambda i: (layer, 0, 0), pipeline_mode=resident)
    row = lambda a: a.reshape(1, D)

    args, in_specs = [x], [tok(D)]
    if attn_out is not None:
        dn, w_o = attn_out
        args += [dn, w_o]
        in_specs += [tok(dn.shape[1]), whole(w_o)]
    args += [row(g), w_gu_all, w_down_all]
    in_specs += [vec, layer_of(w_gu_all), layer_of(w_down_all)]

    kind = post[0]
    transposed_q = False
    if kind == "rg_in":
        _, mix_g, w_in = post
        N = w_in.shape[1] // 2
        args += [row(mix_g), w_in]
        in_specs += [vec, whole(w_in)]
        out_shape = (jax.ShapeDtypeStruct((T, D), F32),) + (jax.ShapeDtypeStruct((T, N), F32),) * 2
        out_specs = (tok(D), tok(N), tok(N))
    elif kind == "q":
        _, mix_g, w_q, tabs, seq_shape = post
        args += [row(mix_g), w_q, *tabs]
        in_specs += [vec, whole(w_q)] + _table_specs(tm, tabs[0].shape[0] // tm)
        transposed_q = seq_shape is not None
        if transposed_q:
            q_shape, q_spec = _head_transposed_out(*seq_shape, tm, BF16)
        else:
            q_shape, q_spec = jax.ShapeDtypeStruct((T, w_q.shape[1]), F32), tok(w_q.shape[1])
        out_shape = (jax.ShapeDtypeStruct((T, D), F32), q_shape)
        out_specs = (tok(D), q_spec)
    else:
        _, p_all, ple_g, w_gate_all, w_proj_all, fin_g = post
        P = p_all.shape[2]
        args += [p_all, row(ple_g), w_gate_all, w_proj_all, row(fin_g)]
        in_specs += [pl.BlockSpec((None, tm, P), lambda i: (layer, i, 0)), vec,
                     layer_of(w_gate_all), layer_of(w_proj_all), vec]
        out_shape = jax.ShapeDtypeStruct((T, D), F32)
        out_specs = tok(D)
    return pl.pallas_call(
        functools.partial(_block_body, pre=None if attn_out is None else "attn_out", post=kind,
                          n_split=_ffn_split(d_ff), transposed_q=transposed_q, final=final),
        out_shape=out_shape,
        grid=(T // tm,),
        in_specs=in_specs,
        out_specs=out_specs,
        compiler_params=_cparams(("parallel",)),
        name="ffn_" + kind,
    )(*args)


def _rg_coeffs(conv, wai_ref, ba, bi, sp):
    cb = conv.astype(BF16)
    nb = wai_ref.shape[0]
    blk = conv.shape[1] // nb
    a_parts, b_parts = [], []
    for n in range(nb):
        sl = slice(n * blk, (n + 1) * blk)
        g = jnp.dot(cb[:, sl], wai_ref[n], preferred_element_type=F32)
        r = jax.nn.sigmoid(g[:, :blk] + ba[:, sl])
        i = jax.nn.sigmoid(g[:, blk:] + bi[:, sl])
        log_a = -RG_C * r * sp[:, sl]
        a = jnp.exp(log_a)
        mult = jnp.sqrt(-jnp.tanh(log_a) * (a * a + 1.0))
        a_parts.append(a)
        b_parts.append(mult * (i * conv[:, sl]))
    return a_parts, b_parts


def _rg_prompt_body(gate_ref, u_ref, x_ref, cw_ref, cb_ref, wai_ref, ba_ref, bi_ref, lam_ref, wo_ref,
                    o_ref, nconv_ref, nh_ref, ubuf_ref, a_ref, h_ref, hstate_ref):
    s = pl.program_id(1)
    ts, D = u_ref.shape[1], u_ref.shape[2]

    @pl.when(s == 0)
    def _():
        ubuf_ref[0:SUBLANES, :] = jnp.zeros((SUBLANES, D), F32)
        hstate_ref[...] = jnp.zeros_like(hstate_ref)

    u = u_ref[0]
    ubuf_ref[SUBLANES:SUBLANES + ts, :] = u
    cw = cw_ref[...]
    conv = cb_ref[...] + u * cw[CONV_W - 1:CONV_W, :]
    for k in range(1, CONV_W):
        conv = conv + ubuf_ref[SUBLANES - k:SUBLANES - k + ts, :] * cw[CONV_W - 1 - k:CONV_W - k, :]

    sp = _softplus(-lam_ref[...])
    a, b = _rg_coeffs(conv, wai_ref, ba_ref[...], bi_ref[...], sp)
    a_ref[...] = jnp.concatenate(a, axis=1)
    h_ref[...] = jnp.concatenate(b, axis=1)

    row = lax.broadcasted_iota(jnp.int32, (SUBLANES, D), 0)

    def group(gi, h_prev):
        r0 = pl.multiple_of(gi * SUBLANES, SUBLANES)
        ag = a_ref[pl.ds(r0, SUBLANES), :]
        bg = h_ref[pl.ds(r0, SUBLANES), :]
        for d in (1, 2, 4):
            m = row >= d
            a_sh = pltpu.roll(ag, d, 0)
            b_sh = pltpu.roll(bg, d, 0)
            bg = jnp.where(m, ag * b_sh + bg, bg)
            ag = jnp.where(m, ag * a_sh, ag)
        hg = ag * h_prev + bg
        h_ref[pl.ds(r0, SUBLANES), :] = hg
        return jnp.broadcast_to(hg[SUBLANES - 1:SUBLANES, :], (SUBLANES, D))

    h_last = lax.fori_loop(0, ts // SUBLANES, group, hstate_ref[...])
    hstate_ref[...] = h_last
    ubuf_ref[0:SUBLANES, :] = u[ts - SUBLANES:ts, :]

    y = (_gelu_tanh(gate_ref[0]) * h_ref[...]).astype(BF16)
    o_ref[0] = x_ref[0] + jnp.dot(y, wo_ref[...], preferred_element_type=F32)

    @pl.when(s == pl.num_programs(1) - 1)
    def _():
        nconv_ref[0] = u[ts - (CONV_W - 1):ts, :]
        nh_ref[0] = h_last[0:1, :]


def _rg_prompt(gate, u, x, cw, cb, wai, ba, bi, lam, wo):
    B, S, D = u.shape
    ts = min(SEQ_TILE, S)
    row = lambda a: a.reshape(1, D)
    tok = pl.BlockSpec((1, ts, D), lambda b, s: (b, s, 0))
    vec = pl.BlockSpec((1, D), lambda b, s: (0, 0))
    return pl.pallas_call(
        _rg_prompt_body,
        out_shape=(jax.ShapeDtypeStruct((B, S, D), F32),
                   jax.ShapeDtypeStruct((B, CONV_W - 1, D), F32),
                   jax.ShapeDtypeStruct((B, 1, D), F32)),
        grid=(B, S // ts),
        in_specs=[tok, tok, tok,
                  pl.BlockSpec((CONV_W, D), lambda b, s: (0, 0)), vec,
                  pl.BlockSpec(wai.shape, lambda b, s: (0, 0, 0)), vec, vec, vec,
                  pl.BlockSpec((D, D), lambda b, s: (0, 0))],
        out_specs=(tok,
                   pl.BlockSpec((1, CONV_W - 1, D), lambda b, s: (b, 0, 0)),
                   pl.BlockSpec((1, 1, D), lambda b, s: (b, 0, 0))),
        scratch_shapes=[pltpu.VMEM((ts + SUBLANES, D), F32), pltpu.VMEM((ts, D), F32),
                        pltpu.VMEM((ts, D), F32), pltpu.VMEM((SUBLANES, D), F32)],
        compiler_params=_cparams(("parallel", "arbitrary")),
        name="rg_prompt",
    )(gate, u, x, cw, row(cb), wai, row(ba), row(bi), row(lam), wo)


def _rg_sample_body(gate_ref, u_ref, x_ref, cprev_ref, h0_ref, cw_ref, cb_ref, wai_ref, ba_ref, bi_ref,
                    lam_ref, wo_ref, o_ref, nconv_ref, nh_ref):
    n_t = u_ref.shape[0]
    cw = cw_ref[...]
    hist = [cprev_ref[k] for k in range(CONV_W - 1)] + [u_ref[t] for t in range(n_t)]
    sp = _softplus(-lam_ref[...])
    h = h0_ref[...]
    for t in range(n_t):
        conv = cb_ref[...] + hist[t] * cw[0:1, :]
        for k in range(1, CONV_W):
            conv = conv + hist[t + k] * cw[k:k + 1, :]
        a, b = _rg_coeffs(conv, wai_ref, ba_ref[...], bi_ref[...], sp)
        h = jnp.concatenate(a, axis=1) * h + jnp.concatenate(b, axis=1)
        y = (_gelu_tanh(gate_ref[t]) * h).astype(BF16)
        o_ref[t] = x_ref[t] + jnp.dot(y, wo_ref[...], preferred_element_type=F32)
    for k in range(CONV_W - 1):
        nconv_ref[k] = hist[n_t + k]
    nh_ref[...] = h


def _rg_sample(gate, u, x, cprev, h0, cw, cb, wai, ba, bi, lam, wo):
    n_t, Bd, D = u.shape
    row = lambda a: a.reshape(1, D)
    return pl.pallas_call(
        _rg_sample_body,
        out_shape=(jax.ShapeDtypeStruct((n_t, Bd, D), F32),
                   jax.ShapeDtypeStruct((CONV_W - 1, Bd, D), F32),
                   jax.ShapeDtypeStruct((Bd, D), F32)),
        compiler_params=pltpu.CompilerParams(vmem_limit_bytes=VMEM_LIMIT),
        name="rg_sample",
    )(gate, u, x, cprev, h0, cw, row(cb), wai, row(ba), row(bi), row(lam), wo)


def _diff_lambda(lq1, lk1, lq2, lk2, lambda_init):
    return (jnp.exp(jnp.sum(lq1 * lk1, axis=-1, keepdims=True))
            - jnp.exp(jnp.sum(lq2 * lk2, axis=-1, keepdims=True)) + lambda_init)


def _diff_norm(o0, o1, lam, subln, lambda_init):
    d = o0 - lam * o1
    return _rms(d, subln) * (1.0 - lambda_init)


def _flash_body(qt_ref, k_ref, vt_ref, lq1_ref, lk1_ref, lq2_ref, lk2_ref, sub_ref, o_ref, w_ref, acc_ref,
                m_ref, *, lambda_init):
    i = pl.program_id(2)
    n_heads = qt_ref.shape[1]
    tq = qt_ref.shape[4]
    chunk = vt_ref.shape[4]

    for hh in range(n_heads):
        qt = qt_ref[0, hh, 0]
        feat = lax.broadcasted_iota(jnp.int32, qt.shape, 0)
        zero = jnp.zeros_like(qt)
        w_ref[hh, :, 0:tq] = jnp.where(feat < HEAD_DIM, qt, zero)
        w_ref[hh, :, tq:2 * tq] = jnp.where(feat < HEAD_DIM, zero, qt)
    acc_ref[...] = jnp.zeros_like(acc_ref)
    m_ref[...] = jnp.full_like(m_ref, NEG_BIG)

    def attend(hh, first, n_full, diagonal):
        cs = [first + t for t in range(n_full)] + ([i] if diagonal else [])
        s = []
        for t, c in enumerate(cs):
            r0 = pl.multiple_of(c * chunk, chunk)
            st = jnp.dot(k_ref[0, pl.ds(r0, chunk), V_DIM * hh:V_DIM * (hh + 1)], w_ref[hh],
                         preferred_element_type=F32)
            if diagonal and t == n_full:
                key = lax.broadcasted_iota(jnp.int32, st.shape, 0)
                qry = lax.broadcasted_iota(jnp.int32, st.shape, 1)
                qry = jnp.where(qry >= tq, qry - tq, qry)
                st = jnp.where(key <= qry, st, NEG_BIG)
            s.append(st)
        m_prev = m_ref[hh]
        m_new = m_prev
        for st in s:
            m_new = jnp.maximum(m_new, jnp.max(st, axis=0, keepdims=True))
        alpha = jnp.exp2(m_prev - m_new)
        pv = None
        for st, c in zip(s, cs):
            p = jnp.exp2(st - m_new).astype(BF16)
            part = jnp.dot(vt_ref[0, hh, c], p, preferred_element_type=F32)
            pv = part if pv is None else pv + part
        acc_ref[hh] = alpha * acc_ref[hh] + pv
        m_ref[hh] = m_new

    def main(g, carry):
        for hh in range(n_heads):
            attend(hh, g * ATTN_JOINT, ATTN_JOINT, False)
        return carry

    lax.fori_loop(0, i // ATTN_JOINT, main, 0)
    for r in range(ATTN_JOINT):
        @pl.when(i % ATTN_JOINT == r)
        def _():
            for hh in range(n_heads):
                attend(hh, i - r, r, True)

    lam = _diff_lambda(lq1_ref[...], lk1_ref[...], lq2_ref[...], lk2_ref[...], lambda_init)
    for hh in range(n_heads):
        acc = acc_ref[hh]
        o = acc[0:V_DIM] / acc[V_DIM:V_DIM + 1]
        d = (o[:, 0:tq] - lam * o[:, tq:2 * tq]).T
        o_ref[0, :, V_DIM * hh:V_DIM * (hh + 1)] = (_rms(d, sub_ref[...]) * (1.0 - lambda_init)).astype(o_ref.dtype)


def _flash_prompt(qt, kb, vt, lq1, lk1, lq2, lk2, subln, lambda_init):
    B, S, W = kb.shape
    n_chunks, chunk = qt.shape[2], qt.shape[4]
    hp = ATTN_HEADS_PER_STEP
    vec = lambda n: pl.BlockSpec((1, n), lambda b, h, i: (0, 0))
    row = lambda a: a.reshape(1, -1)
    return pl.pallas_call(
        functools.partial(_flash_body, lambda_init=lambda_init),
        out_shape=jax.ShapeDtypeStruct((B, S, W), BF16),
        grid=(B, N_HEADS // hp, n_chunks),
        in_specs=[
            pl.BlockSpec((1, hp, 1, V_DIM, chunk), lambda b, h, i: (b, h, i, 0, 0)),
            pl.BlockSpec((1, S, hp * V_DIM), lambda b, h, i: (b, 0, h)),
            pl.BlockSpec((1, hp, n_chunks, vt.shape[3], chunk), lambda b, h, i: (b, h, 0, 0, 0)),
            vec(HEAD_DIM), vec(HEAD_DIM), vec(HEAD_DIM), vec(HEAD_DIM), vec(V_DIM),
        ],
        out_specs=pl.BlockSpec((1, chunk, hp * V_DIM), lambda b, h, i: (b, i, h)),
        scratch_shapes=[pltpu.VMEM((hp, V_DIM, 2 * chunk), BF16), pltpu.VMEM((hp, vt.shape[3], 2 * chunk), F32),
                        pltpu.VMEM((hp, 1, 2 * chunk), F32)],
        compiler_params=_cparams(("parallel", "parallel", "arbitrary")),
        name="flash_diff_attn",
    )(qt, kb, vt, row(lq1), row(lk1), row(lq2), row(lk2), row(subln))


def _paged_body(pt_ref, q_ref, kn_ref, vn_ref, *rest, n_pp, lambda_init):
    kt_refs = rest[:n_pp]
    v_refs = rest[n_pp:2 * n_pp]
    lq1_ref, lk1_ref, lq2_ref, lk2_ref, sub_ref, o_ref, qblk_ref, m_ref, l_ref, acc_ref = rest[2 * n_pp:]
    j = pl.program_id(1)
    n_q = q_ref.shape[1]
    page = kt_refs[0].shape[2]
    grp = 2 * n_q

    def update(s, head_pv):
        m_prev = m_ref[...]
        m_new = jnp.maximum(m_prev, jnp.max(s, axis=1, keepdims=True))
        alpha = jnp.exp(m_prev - m_new)
        p = jnp.exp(s - m_new)
        l_ref[...] = alpha * l_ref[...] + jnp.sum(p, axis=1, keepdims=True)
        m_ref[...] = m_new
        for h in range(N_HEADS):
            sl = slice(grp * h, grp * (h + 1))
            acc_ref[sl, :] = alpha[sl] * acc_ref[sl, :] + head_pv(p[sl], h)

    @pl.when(j == 0)
    def _():
        q = q_ref[0]
        r = lax.broadcasted_iota(jnp.int32, qblk_ref.shape, 0)
        c = lax.broadcasted_iota(jnp.int32, qblk_ref.shape, 1)
        qrep = jnp.zeros(qblk_ref.shape, F32)
        for t in range(n_q):
            qrep = jnp.where(r % n_q == t, jnp.broadcast_to(q[t:t + 1, :], qblk_ref.shape), qrep)
        qblk_ref[...] = jnp.where(c // HEAD_DIM == r // n_q, qrep, 0.0)
        m_ref[...] = jnp.full_like(m_ref, NEG_BIG)
        l_ref[...] = jnp.zeros_like(l_ref)
        acc_ref[...] = jnp.zeros_like(acc_ref)
        kn, vn = kn_ref[0], vn_ref[0]
        s = lax.dot_general(qblk_ref[...], kn, (((1,), (1,)), ((), ())), preferred_element_type=F32)
        rn = lax.broadcasted_iota(jnp.int32, s.shape, 0)
        cn = lax.broadcasted_iota(jnp.int32, s.shape, 1)
        s = jnp.where(cn <= rn % n_q, s, NEG_BIG)
        update(s, lambda p, h: jnp.dot(p, vn[:, V_DIM * h:V_DIM * (h + 1)], preferred_element_type=F32))

    qblk = qblk_ref[...]
    s = jnp.concatenate([jnp.dot(qblk, kt_ref[0], preferred_element_type=F32) for kt_ref in kt_refs], axis=1)

    def head_pv(p, h):
        out = None
        for g, v_ref in enumerate(v_refs):
            part = jnp.dot(p[:, page * g:page * (g + 1)], v_ref[0, pl.ds(h, page, stride=N_HEADS), :],
                           preferred_element_type=F32)
            out = part if out is None else out + part
        return out

    update(s, head_pv)

    @pl.when(j == pl.num_programs(1) - 1)
    def _():
        o = acc_ref[...] / l_ref[...]
        lam = _diff_lambda(lq1_ref[...], lk1_ref[...], lq2_ref[...], lk2_ref[...], lambda_init)
        for h in range(N_HEADS):
            blk = o[grp * h:grp * (h + 1)]
            o_ref[0, n_q * h:n_q * (h + 1), :] = _diff_norm(blk[0:n_q], blk[n_q:grp], lam, sub_ref[...], lambda_init)


def _paged_attn(q, k_new, v_new, cache_k, cache_v, page_table, lq1, lk1, lq2, lk2, subln, lambda_init):
    Bd, n_q, W = q.shape
    n_pool, page = cache_k.shape[0], cache_k.shape[1]
    n_pages = page_table.shape[1]
    n_pp = next(n for n in (PAGES_PER_STEP, 4, 2, 1) if n_pages % n == 0)
    n_steps = n_pages // n_pp
    rows = 2 * N_HEADS * n_q
    ckt = cache_k.transpose(0, 2, 3, 1).reshape(n_pool, W, page)
    cv = cache_v.reshape(n_pool, page * N_HEADS, V_DIM)
    pt = page_table.reshape(-1)

    def page_spec(shape, i):
        return pl.BlockSpec((1,) + shape[1:], lambda b, j, pt: (pt[b * n_pages + j * n_pp + i], 0, 0))

    per_seq = lambda a: pl.BlockSpec((1,) + a.shape[1:], lambda b, j, pt: (b, 0, 0))
    vec = lambda n: pl.BlockSpec((1, n), lambda b, j, pt: (0, 0))
    row = lambda a: a.reshape(1, -1)
    return pl.pallas_call(
        functools.partial(_paged_body, n_pp=n_pp, lambda_init=lambda_init),
        out_shape=jax.ShapeDtypeStruct((Bd, N_HEADS * n_q, V_DIM), F32),
        grid_spec=pltpu.PrefetchScalarGridSpec(
            num_scalar_prefetch=1,
            grid=(Bd, n_steps),
            in_specs=[per_seq(q), per_seq(k_new), per_seq(v_new)]
                     + [page_spec(ckt.shape, i) for i in range(n_pp)]
                     + [page_spec(cv.shape, i) for i in range(n_pp)]
                     + [vec(HEAD_DIM)] * 4 + [vec(V_DIM)],
            out_specs=pl.BlockSpec((1, N_HEADS * n_q, V_DIM), lambda b, j, pt: (b, 0, 0)),
            scratch_shapes=[pltpu.VMEM((rows, W), F32), pltpu.VMEM((rows, 1), F32),
                            pltpu.VMEM((rows, 1), F32), pltpu.VMEM((rows, V_DIM), F32)],
        ),
        compiler_params=_cparams(("parallel", "arbitrary")),
        name="paged_diff_attn",
    )(pt, q, k_new, v_new, *([ckt] * n_pp), *([cv] * n_pp), row(lq1), row(lk1), row(lq2), row(lk2), row(subln))


def _rope_tables(pos):
    half = HEAD_DIM // 2
    lane = jnp.arange(LANES)
    inv = jnp.power(ROPE_THETA, -(lane % half).astype(F32) * 2.0 / HEAD_DIM)
    ang = pos.astype(F32)[:, None] * inv[None, :]
    cos, sin = jnp.cos(ang), jnp.sin(ang)
    upper = (lane % HEAD_DIM) >= half
    return cos, jnp.where(upper, sin, 0.0), jnp.where(upper, 0.0, -sin)


def kernel(x_prompt, x_sample, p_prompt, p_sample, cache_k, cache_v, page_table, state_conv, state_rglru, ffn1_norm, ffn1_w_gu, ffn1_w_down, mix_norm, rg_w_in, rg_conv_w, rg_conv_b, rg_w_a, rg_b_a, rg_w_i, rg_b_i, rg_lambda, rg_w_out, kv_norm, w_kv, attn_w_q, lambda_q1, lambda_k1, lambda_q2, lambda_k2, attn_subln, attn_w_o, ffn2_norm, ffn2_w_gu, ffn2_w_down, ple_norm, ple_w_gate, ple_w_proj, final_norm):
    B, S, D = x_prompt.shape
    Bd, Sd, _ = x_sample.shape
    depth = ffn1_norm.shape[0]
    n_a = rg_w_in.shape[0]
    assert depth == 2 and n_a == 1, "one recurrent layer followed by one attention layer"
    assert D == N_HEADS * V_DIM
    n_pages, page = page_table.shape[1], cache_k.shape[1]
    past_len = n_pages * page
    kd = 2 * N_HEADS * HEAD_DIM
    lambda_init = 0.8 - 0.6 * math.exp(-0.3 * 1)

    bf = lambda w: w.astype(BF16)
    w_gu1, w_dn1, w_gu2, w_dn2 = bf(ffn1_w_gu), bf(ffn1_w_down), bf(ffn2_w_gu), bf(ffn2_w_down)
    w_in, w_out, w_kvb, w_q, w_o = bf(rg_w_in[0]), bf(rg_w_out[0]), bf(w_kv), bf(attn_w_q[0]), bf(attn_w_o[0])
    w_pg, w_pp = bf(ple_w_gate), bf(ple_w_proj)
    wai = bf(jnp.concatenate([rg_w_a[0], rg_w_i[0]], axis=-1))

    def layer0_pre(x):
        return _ffn_block(x, ffn1_norm[0], w_gu1, w_dn1, 0, ("rg_in", mix_norm[0], w_in))

    def layer0_post(x, p_all, tabs, seq_shape):
        x = _ffn_block(x, ffn2_norm[0], w_gu2, w_dn2, 0, ("ple", p_all, ple_norm[0], w_pg, w_pp, final_norm))
        kv = _kv_proj(x, kv_norm, w_kvb, tabs, kd, seq_shape)
        x, q = _ffn_block(x, ffn1_norm[1], w_gu1, w_dn1, 1, ("q", mix_norm[1], w_q, tabs, seq_shape))
        return x, q, kv

    def layer1_post(x, dn, p_all):
        return _ffn_block(x, ffn2_norm[1], w_gu2, w_dn2, 1, ("ple", p_all, ple_norm[1], w_pg, w_pp, final_norm),
                          attn_out=(dn, w_o), final=True)

    rg_args = (rg_conv_w[0], rg_conv_b[0], wai, rg_b_a[0], rg_b_i[0], rg_lambda[0], w_out)
    lam_args = (lambda_q1[0], lambda_k1[0], lambda_q2[0], lambda_k2[0], attn_subln[0], lambda_init)

    Tp = B * S
    tabs_p = _rope_tables(jnp.arange(S))
    p_p = p_prompt.reshape(depth, Tp, -1)
    x, gate, u = layer0_pre(x_prompt.reshape(Tp, D))
    x, new_conv_p, new_h_p = _rg_prompt(gate.reshape(B, S, D), u.reshape(B, S, D), x.reshape(B, S, D), *rg_args)
    x, qt, (k_p, v_p, kb, vt) = layer0_post(x.reshape(Tp, D), p_p, tabs_p, (B, S))
    dn = _flash_prompt(qt, kb.reshape(B, S, D), vt, *lam_args)
    y_prompt = layer1_post(x, dn.reshape(Tp, D), p_p).reshape(B, S, D)

    Ts = Sd * Bd
    tm = lambda a: jnp.swapaxes(a, 0, 1)
    tabs_s = _rope_tables(jnp.repeat(past_len + jnp.arange(Sd), Bd))
    p_s = jnp.swapaxes(p_sample, 1, 2).reshape(depth, Ts, -1)
    x, gate, u = layer0_pre(tm(x_sample).reshape(Ts, D))
    x, new_conv_s, new_h_s = _rg_sample(gate.reshape(Sd, Bd, D), u.reshape(Sd, Bd, D), x.reshape(Sd, Bd, D),
                                        tm(state_conv[0]), state_rglru[0], *rg_args)
    x, q, (k_s, v_s) = layer0_post(x.reshape(Ts, D), p_s, tabs_s, None)
    bm = lambda a: tm(a.reshape(Sd, Bd, -1))
    k_s, v_s = bm(k_s), bm(v_s)
    pad = lambda a: jnp.pad(a, ((0, 0), (0, -Sd % SUBLANES), (0, 0)))
    dn = _paged_attn(bm(q), pad(k_s), pad(v_s), cache_k, cache_v, page_table, *lam_args)
    dn = dn.reshape(Bd, N_HEADS, Sd, V_DIM).transpose(2, 0, 1, 3).reshape(Ts, D)
    y_sample = bm(layer1_post(x, dn, p_s))

    return (y_prompt, y_sample,
            k_p.reshape(B, 2 * N_HEADS, HEAD_DIM, S).transpose(0, 3, 1, 2), v_p.reshape(B, S, N_HEADS, V_DIM),
            k_s.reshape(Bd, Sd, 2 * N_HEADS, HEAD_DIM), v_s.reshape(Bd, Sd, N_HEADS, V_DIM),
            new_conv_p[None], new_h_p.reshape(1, B, D),
            tm(new_conv_s)[None], new_h_s[None])
```

```python
import functools
import math

import jax
import jax.numpy as jnp
from jax import lax
from jax.experimental import pallas as pl
from jax.experimental.pallas import tpu as pltpu

F32 = jnp.float32
BF16 = jnp.bfloat16

EPS = 1e-6
N_HEADS = 8
HEAD_DIM = 64
V_DIM = 2 * HEAD_DIM
N_RG_BLOCKS = 8
CONV_W = 4
RG_C = 8.0
ROPE_THETA = 10000.0
LANES = 128
SUBLANES = 8
NEG_BIG = -1e30
LOG2_E = math.log2(math.e)
ONES_ROWS = 16
VMEM_LIMIT = 56 * 1024 * 1024

TOKEN_TILE = 512
SEQ_TILE = 512
ATTN_CHUNK = 256
ATTN_JOINT = 4
ATTN_HEADS_PER_STEP = 4


def _cparams(sem):
    return pltpu.CompilerParams(dimension_semantics=sem, vmem_limit_bytes=VMEM_LIMIT)


def _rms(x, g):
    return x * lax.rsqrt(jnp.mean(x * x, axis=-1, keepdims=True) + EPS) * g


def _rope(x, cos, sin_hi, sin_lo):
    outs = []
    for j in range(x.shape[1] // LANES):
        xj = x[:, j * LANES:(j + 1) * LANES]
        outs.append(xj * cos
                    + pltpu.roll(xj, LANES - HEAD_DIM // 2, 1) * sin_lo
                    + pltpu.roll(xj, HEAD_DIM // 2, 1) * sin_hi)
    return jnp.concatenate(outs, axis=1)


def _gelu_tanh(x):
    return 0.5 * x * (1.0 + jnp.tanh(math.sqrt(2.0 / math.pi) * (x + 0.044715 * (x * x * x))))


def _softplus(z):
    return jnp.maximum(z, 0.0) + jnp.log1p(jnp.exp(-jnp.abs(z)))


def _ffn_split(d_ff):
    for n_split in (2, 4, 1):
        if d_ff % (n_split * LANES) == 0:
            return n_split
    raise ValueError(f"unsupported FFN width {d_ff}")


def _store_head_transposed(dst_ref, x):
    n_chunks, n_rows, chunk = dst_ref.shape[2], dst_ref.shape[3], dst_ref.shape[4]
    for h in range(N_HEADS):
        xt = x[:, h * V_DIM:(h + 1) * V_DIM].T.astype(dst_ref.dtype)
        for c in range(n_chunks):
            dst_ref[0, h, c, 0:V_DIM, :] = xt[:, c * chunk:(c + 1) * chunk]
            if n_rows > V_DIM:
                dst_ref[0, h, c, V_DIM:n_rows, :] = jnp.ones((n_rows - V_DIM, chunk), dst_ref.dtype)


def _head_transposed_out(B, S, tm, dtype, extra_rows=0):
    tiles_per_seq = S // tm
    rows = V_DIM + extra_rows
    shape = jax.ShapeDtypeStruct((B, N_HEADS, S // ATTN_CHUNK, rows, ATTN_CHUNK), dtype)
    spec = pl.BlockSpec((1, N_HEADS, tm // ATTN_CHUNK, rows, ATTN_CHUNK),
                        lambda i: (i // tiles_per_seq, 0, i % tiles_per_seq, 0, 0))
    return shape, spec


def _table_specs(tm, n_tab_tiles):
    return [pl.BlockSpec((tm, LANES), lambda i: (i % n_tab_tiles, 0))] * 3


def _kv_body(x_ref, g_ref, w_ref, cos_ref, shi_ref, slo_ref, k_ref, v_ref, *lowp_refs):
    n = _rms(x_ref[...], g_ref[...]).astype(BF16)
    kv = jnp.dot(n, w_ref[...], preferred_element_type=F32)
    kd = N_HEADS * V_DIM
    k = _rope(kv[:, :kd], cos_ref[...], shi_ref[...], slo_ref[...])
    v = kv[:, kd:]
    if not lowp_refs:
        k_ref[...] = k
        v_ref[...] = v
    else:
        for h in range(N_HEADS):
            k_ref[0, h] = k[:, h * V_DIM:(h + 1) * V_DIM].T
        v_ref[...] = v
        kb_ref, vt_ref = lowp_refs
        kb_ref[...] = k.astype(kb_ref.dtype)
        _store_head_transposed(vt_ref, v)


def _kv_proj(x, g, w_kv, tabs, kd, seq_shape=None):
    T, D = x.shape
    N = w_kv.shape[1]
    vd = N - kd
    tm = min(TOKEN_TILE, T)
    out_shape = [jax.ShapeDtypeStruct((T, kd), F32), jax.ShapeDtypeStruct((T, vd), F32)]
    out_specs = [pl.BlockSpec((tm, kd), lambda i: (i, 0)), pl.BlockSpec((tm, vd), lambda i: (i, 0))]
    if seq_shape is not None:
        B, S = seq_shape
        tiles_per_seq = S // tm
        out_shape[0] = jax.ShapeDtypeStruct((B, N_HEADS, V_DIM, S), F32)
        out_specs[0] = pl.BlockSpec((1, N_HEADS, V_DIM, tm), lambda i: (i // tiles_per_seq, 0, 0, i % tiles_per_seq))
        vt_shape, vt_spec = _head_transposed_out(B, S, tm, BF16, extra_rows=ONES_ROWS)
        out_shape += [jax.ShapeDtypeStruct((T, kd), BF16), vt_shape]
        out_specs += [pl.BlockSpec((tm, kd), lambda i: (i, 0)), vt_spec]
    return pl.pallas_call(
        _kv_body,
        out_shape=tuple(out_shape),
        grid=(T // tm,),
        in_specs=[
            pl.BlockSpec((tm, D), lambda i: (i, 0)),
            pl.BlockSpec((1, D), lambda i: (0, 0)),
            pl.BlockSpec((D, N), lambda i: (0, 0)),
        ] + _table_specs(tm, tabs[0].shape[0] // tm),
        out_specs=tuple(out_specs),
        compiler_params=_cparams(("parallel",)),
        name="kv_proj",
    )(x, g.reshape(1, D), w_kv, *tabs)


def _block_body(*refs, pre, post, n_split, transposed_q, final):
    refs = list(refs)
    take = lambda k: [refs.pop(0) for _ in range(k)]
    (x_ref,) = take(1)
    x = x_ref[...]
    if pre == "attn_out":
        dn_ref, wo_ref = take(2)
        x = x + jnp.dot(dn_ref[...].astype(BF16), wo_ref[...], preferred_element_type=F32)
    g_ref, wgu_ref, wd_ref = take(3)

    n = _rms(x, g_ref[...]).astype(BF16)
    d_ff = wd_ref.shape[0]
    tf = d_ff // n_split
    y = None
    for c in range(n_split):
        g = jnp.dot(n, wgu_ref[:, c * tf:(c + 1) * tf], preferred_element_type=F32)
        u = jnp.dot(n, wgu_ref[:, d_ff + c * tf:d_ff + (c + 1) * tf], preferred_element_type=F32)
        h = (g * jax.nn.sigmoid(g) * u).astype(BF16)
        part = jnp.dot(h, wd_ref[c * tf:(c + 1) * tf, :], preferred_element_type=F32)
        y = part if y is None else y + part
    x = x + 0.5 * y

    if post == "rg_in":
        mg_ref, win_ref, x_out, gate_ref, u_ref = take(5)
        x_out[...] = x
        z = jnp.dot(_rms(x, mg_ref[...]).astype(BF16), win_ref[...], preferred_element_type=F32)
        d = gate_ref.shape[1]
        gate_ref[...] = z[:, :d]
        u_ref[...] = z[:, d:]
    elif post == "q":
        mg_ref, wq_ref, cos_ref, shi_ref, slo_ref, x_out, q_ref = take(7)
        x_out[...] = x
        q = jnp.dot(_rms(x, mg_ref[...]).astype(BF16), wq_ref[...], preferred_element_type=F32)
        q = _rope(q, cos_ref[...], shi_ref[...], slo_ref[...]) * (HEAD_DIM ** -0.5)
        if transposed_q:
            _store_head_transposed(q_ref, q * LOG2_E)
        else:
            q_ref[...] = q
    else:
        p_ref, pg_ref, wpg_ref, wpp_ref, fin_ref, x_out = take(6)
        gate = jax.nn.sigmoid(jnp.dot(_rms(x, pg_ref[...]).astype(BF16), wpg_ref[...], preferred_element_type=F32))
        x = x + gate * jnp.dot(p_ref[...].astype(BF16), wpp_ref[...], preferred_element_type=F32)
        x_out[...] = _rms(x, fin_ref[...]) if final else x


def _ffn_block(x, g, w_gu_all, w_down_all, layer, post, *, attn_out=None, final=False):
    T, D = x.shape
    d_ff = w_down_all.shape[1]
    tm = min(TOKEN_TILE, T)
    resident = pl.Buffered(1)
    tok = lambda width: pl.BlockSpec((tm, width), lambda i: (i, 0))
    vec = pl.BlockSpec((1, D), lambda i: (0, 0))
    whole = lambda a: pl.BlockSpec(a.shape, lambda i: (0,) * a.ndim, pipeline_mode=resident)
    layer_of = lambda a: pl.BlockSpec((None,) + a.shape[1:], lambda i: (layer, 0, 0), pipeline_mode=resident)
    row = lambda a: a.reshape(1, D)

    args, in_specs = [x], [tok(D)]
    if attn_out is not None:
        dn, w_o = attn_out
        args += [dn, w_o]
        in_specs += [tok(dn.shape[1]), whole(w_o)]
    args += [row(g), w_gu_all, w_down_all]
    in_specs += [vec, layer_of(w_gu_all), layer_of(w_down_all)]

    kind = post[0]
    transposed_q = False
    if kind == "rg_in":
        _, mix_g, w_in = post
        N = w_in.shape[1] // 2
        args += [row(mix_g), w_in]
        in_specs += [vec, whole(w_in)]
        out_shape = (jax.ShapeDtypeStruct((T, D), F32),) + (jax.ShapeDtypeStruct((T, N), F32),) * 2
        out_specs = (tok(D), tok(N), tok(N))
    elif kind == "q":
        _, mix_g, w_q, tabs, seq_shape = post
        args += [row(mix_g), w_q, *tabs]
        in_specs += [vec, whole(w_q)] + _table_specs(tm, tabs[0].shape[0] // tm)
        transposed_q = seq_shape is not None
        if transposed_q:
            q_shape, q_spec = _head_transposed_out(*seq_shape, tm, BF16)
        else:
            q_shape, q_spec = jax.ShapeDtypeStruct((T, w_q.shape[1]), F32), tok(w_q.shape[1])
        out_shape = (jax.ShapeDtypeStruct((T, D), F32), q_shape)
        out_specs = (tok(D), q_spec)
    else:
        _, p_all, ple_g, w_gate_all, w_proj_all, fin_g = post
        P = p_all.shape[2]
        args += [p_all, row(ple_g), w_gate_all, w_proj_all, row(fin_g)]
        in_specs += [pl.BlockSpec((None, tm, P), lambda i: (layer, i, 0)), vec,
                     layer_of(w_gate_all), layer_of(w_proj_all), vec]
        out_shape = jax.ShapeDtypeStruct((T, D), F32)
        out_specs = tok(D)
    return pl.pallas_call(
        functools.partial(_block_body, pre=None if attn_out is None else "attn_out", post=kind,
                          n_split=_ffn_split(d_ff), transposed_q=transposed_q, final=final),
        out_shape=out_shape,
        grid=(T // tm,),
        in_specs=in_specs,
        out_specs=out_specs,
        compiler_params=_cparams(("parallel",)),
        name="ffn_" + kind,
    )(*args)


def _rg_coeffs(conv, wai_ref, ba, bi, sp):
    cb = conv.astype(BF16)
    nb = wai_ref.shape[0]
    blk = conv.shape[1] // nb
    a_parts, b_parts = [], []
    for n in range(nb):
        sl = slice(n * blk, (n + 1) * blk)
        g = jnp.dot(cb[:, sl], wai_ref[n], preferred_element_type=F32)
        r = jax.nn.sigmoid(g[:, :blk] + ba[:, sl])
        i = jax.nn.sigmoid(g[:, blk:] + bi[:, sl])
        log_a = -RG_C * r * sp[:, sl]
        a = jnp.exp(log_a)
        mult = jnp.sqrt(-jnp.tanh(log_a) * (a * a + 1.0))
        a_parts.append(a)
        b_parts.append(mult * (i * conv[:, sl]))
    return a_parts, b_parts


def _rg_prompt_body(gate_ref, u_ref, x_ref, cw_ref, cb_ref, wai_ref, ba_ref, bi_ref, lam_ref, wo_ref,
                    o_ref, nconv_ref, nh_ref, ubuf_ref, a_ref, h_ref, hstate_ref):
    s = pl.program_id(1)
    ts, D = u_ref.shape[1], u_ref.shape[2]

    @pl.when(s == 0)
    def _():
        ubuf_ref[0:SUBLANES, :] = jnp.zeros((SUBLANES, D), F32)
        hstate_ref[...] = jnp.zeros_like(hstate_ref)

    u = u_ref[0]
    ubuf_ref[SUBLANES:SUBLANES + ts, :] = u
    cw = cw_ref[...]
    conv = cb_ref[...] + u * cw[CONV_W - 1:CONV_W, :]
    for k in range(1, CONV_W):
        conv = conv + ubuf_ref[SUBLANES - k:SUBLANES - k + ts, :] * cw[CONV_W - 1 - k:CONV_W - k, :]

    sp = _softplus(-lam_ref[...])
    a, b = _rg_coeffs(conv, wai_ref, ba_ref[...], bi_ref[...], sp)
    a_ref[...] = jnp.concatenate(a, axis=1)
    h_ref[...] = jnp.concatenate(b, axis=1)

    row = lax.broadcasted_iota(jnp.int32, (SUBLANES, D), 0)

    def group(gi, h_prev):
        r0 = pl.multiple_of(gi * SUBLANES, SUBLANES)
        ag = a_ref[pl.ds(r0, SUBLANES), :]
        bg = h_ref[pl.ds(r0, SUBLANES), :]
        for d in (1, 2, 4):
            m = row >= d
            a_sh = pltpu.roll(ag, d, 0)
            b_sh = pltpu.roll(bg, d, 0)
            bg = jnp.where(m, ag * b_sh + bg, bg)
            ag = jnp.where(m, ag * a_sh, ag)
        hg = ag * h_prev + bg
        h_ref[pl.ds(r0, SUBLANES), :] = hg
        return jnp.broadcast_to(hg[SUBLANES - 1:SUBLANES, :], (SUBLANES, D))

    h_last = lax.fori_loop(0, ts // SUBLANES, group, hstate_ref[...])
    hstate_ref[...] = h_last
    ubuf_ref[0:SUBLANES, :] = u[ts - SUBLANES:ts, :]

    y = (_gelu_tanh(gate_ref[0]) * h_ref[...]).astype(BF16)
    o_ref[0] = x_ref[0] + jnp.dot(y, wo_ref[...], preferred_element_type=F32)

    @pl.when(s == pl.num_programs(1) - 1)
    def _():
        nconv_ref[0] = u[ts - (CONV_W - 1):ts, :]
        nh_ref[0] = h_last[0:1, :]


def _rg_prompt(gate, u, x, cw, cb, wai, ba, bi, lam, wo):
    B, S, D = u.shape
    ts = min(SEQ_TILE, S)
    row = lambda a: a.reshape(1, D)
    tok = pl.BlockSpec((1, ts, D), lambda b, s: (b, s, 0))
    vec = pl.BlockSpec((1, D), lambda b, s: (0, 0))
    return pl.pallas_call(
        _rg_prompt_body,
        out_shape=(jax.ShapeDtypeStruct((B, S, D), F32),
                   jax.ShapeDtypeStruct((B, CONV_W - 1, D), F32),
                   jax.ShapeDtypeStruct((B, 1, D), F32)),
        grid=(B, S // ts),
        in_specs=[tok, tok, tok,
                  pl.BlockSpec((CONV_W, D), lambda b, s: (0, 0)), vec,
                  pl.BlockSpec(wai.shape, lambda b, s: (0, 0, 0)), vec, vec, vec,
                  pl.BlockSpec((D, D), lambda b, s: (0, 0))],
        out_specs=(tok,
                   pl.BlockSpec((1, CONV_W - 1, D), lambda b, s: (b, 0, 0)),
                   pl.BlockSpec((1, 1, D), lambda b, s: (b, 0, 0))),
        scratch_shapes=[pltpu.VMEM((ts + SUBLANES, D), F32), pltpu.VMEM((ts, D), F32),
                        pltpu.VMEM((ts, D), F32), pltpu.VMEM((SUBLANES, D), F32)],
        compiler_params=_cparams(("parallel", "arbitrary")),
        name="rg_prompt",
    )(gate, u, x, cw, row(cb), wai, row(ba), row(bi), row(lam), wo)


def _rg_sample_body(gate_ref, u_ref, x_ref, cprev_ref, h0_ref, cw_ref, cb_ref, wai_ref, ba_ref, bi_ref,
                    lam_ref, wo_ref, o_ref, nconv_ref, nh_ref):
    n_t = u_ref.shape[0]
    cw = cw_ref[...]
    hist = [cprev_ref[k] for k in range(CONV_W - 1)] + [u_ref[t] for t in range(n_t)]
    sp = _softplus(-lam_ref[...])
    h = h0_ref[...]
    for t in range(n_t):
        conv = cb_ref[...] + hist[t] * cw[0:1, :]
        for k in range(1, CONV_W):
            conv = conv + hist[t + k] * cw[k:k + 1, :]
        a, b = _rg_coeffs(conv, wai_ref, ba_ref[...], bi_ref[...], sp)
        h = jnp.concatenate(a, axis=1) * h + jnp.concatenate(b, axis=1)
        y = (_gelu_tanh(gate_ref[t]) * h).astype(BF16)
        o_ref[t] = x_ref[t] + jnp.dot(y, wo_ref[...], preferred_element_type=F32)
    for k in range(CONV_W - 1):
        nconv_ref[k] = hist[n_t + k]
    nh_ref[...] = h


def _rg_sample(gate, u, x, cprev, h0, cw, cb, wai, ba, bi, lam, wo):
    n_t, Bd, D = u.shape
    row = lambda a: a.reshape(1, D)
    return pl.pallas_call(
        _rg_sample_body,
        out_shape=(jax.ShapeDtypeStruct((n_t, Bd, D), F32),
                   jax.ShapeDtypeStruct((CONV_W - 1, Bd, D), F32),
                   jax.ShapeDtypeStruct((Bd, D), F32)),
        compiler_params=pltpu.CompilerParams(vmem_limit_bytes=VMEM_LIMIT),
        name="rg_sample",
    )(gate, u, x, cprev, h0, cw, row(cb), wai, row(ba), row(bi), row(lam), wo)


def _diff_lambda(lq1, lk1, lq2, lk2, lambda_init):
    return (jnp.exp(jnp.sum(lq1 * lk1, axis=-1, keepdims=True))
            - jnp.exp(jnp.sum(lq2 * lk2, axis=-1, keepdims=True)) + lambda_init)


def _diff_norm(o0, o1, lam, subln, lambda_init):
    d = o0 - lam * o1
    return _rms(d, subln) * (1.0 - lambda_init)


def _flash_body(qt_ref, k_ref, vt_ref, lq1_ref, lk1_ref, lq2_ref, lk2_ref, sub_ref, o_ref, w_ref, acc_ref,
                m_ref, *, lambda_init):
    i = pl.program_id(2)
    n_heads = qt_ref.shape[1]
    tq = qt_ref.shape[4]
    chunk = vt_ref.shape[4]

    for hh in range(n_heads):
        qt = qt_ref[0, hh, 0]
        feat = lax.broadcasted_iota(jnp.int32, qt.shape, 0)
        zero = jnp.zeros_like(qt)
        w_ref[hh, :, 0:tq] = jnp.where(feat < HEAD_DIM, qt, zero)
        w_ref[hh, :, tq:2 * tq] = jnp.where(feat < HEAD_DIM, zero, qt)
    acc_ref[...] = jnp.zeros_like(acc_ref)
    m_ref[...] = jnp.full_like(m_ref, NEG_BIG)

    def attend(hh, first, n_full, diagonal):
        cs = [first + t for t in range(n_full)] + ([i] if diagonal else [])
        s = []
        for t, c in enumerate(cs):
            r0 = pl.multiple_of(c * chunk, chunk)
            st = jnp.dot(k_ref[0, pl.ds(r0, chunk), V_DIM * hh:V_DIM * (hh + 1)], w_ref[hh],
                         preferred_element_type=F32)
            if diagonal and t == n_full:
                key = lax.broadcasted_iota(jnp.int32, st.shape, 0)
                qry = lax.broadcasted_iota(jnp.int32, st.shape, 1)
                qry = jnp.where(qry >= tq, qry - tq, qry)
                st = jnp.where(key <= qry, st, NEG_BIG)
            s.append(st)
        m_prev = m_ref[hh]
        m_new = m_prev
        for st in s:
            m_new = jnp.maximum(m_new, jnp.max(st, axis=0, keepdims=True))
        alpha = jnp.exp2(m_prev - m_new)
        pv = None
        for st, c in zip(s, cs):
            p = jnp.exp2(st - m_new).astype(BF16)
            part = jnp.dot(vt_ref[0, hh, c], p, preferred_element_type=F32)
            pv = part if pv is None else pv + part
        acc_ref[hh] = alpha * acc_ref[hh] + pv
        m_ref[hh] = m_new

    def main(g, carry):
        for hh in range(n_heads):
            attend(hh, g * ATTN_JOINT, ATTN_JOINT, False)
        return carry

    lax.fori_loop(0, i // ATTN_JOINT, main, 0)
    for r in range(ATTN_JOINT):
        @pl.when(i % ATTN_JOINT == r)
        def _():
            for hh in range(n_heads):
                attend(hh, i - r, r, True)

    lam = _diff_lambda(lq1_ref[...], lk1_ref[...], lq2_ref[...], lk2_ref[...], lambda_init)
    for hh in range(n_heads):
        acc = acc_ref[hh]
        o = acc[0:V_DIM] / acc[V_DIM:V_DIM + 1]
        d = (o[:, 0:tq] - lam * o[:, tq:2 * tq]).T
        o_ref[0, :, V_DIM * hh:V_DIM * (hh + 1)] = (_rms(d, sub_ref[...]) * (1.0 - lambda_init)).astype(o_ref.dtype)


def _paged_body(q_ref, kn_ref, vn_ref, kt_refs, v_refs, lam_refs, o_ref, qblk_ref, m_ref, l_ref, acc_ref,
                lambda_init):
    lq1_ref, lk1_ref, lq2_ref, lk2_ref, sub_ref = lam_refs
    n_q = q_ref.shape[1]
    page = kt_refs[0].shape[2]
    grp = 2 * n_q

    def update(s, head_pv):
        m_prev = m_ref[...]
        m_new = jnp.maximum(m_prev, jnp.max(s, axis=1, keepdims=True))
        alpha = jnp.exp(m_prev - m_new)
        p = jnp.exp(s - m_new)
        l_ref[...] = alpha * l_ref[...] + jnp.sum(p, axis=1, keepdims=True)
        m_ref[...] = m_new
        for h in range(N_HEADS):
            sl = slice(grp * h, grp * (h + 1))
            acc_ref[sl, :] = alpha[sl] * acc_ref[sl, :] + head_pv(p[sl], h)

    q = q_ref[0]
    r = lax.broadcasted_iota(jnp.int32, qblk_ref.shape, 0)
    c = lax.broadcasted_iota(jnp.int32, qblk_ref.shape, 1)
    qrep = jnp.zeros(qblk_ref.shape, F32)
    for t in range(n_q):
        qrep = jnp.where(r % n_q == t, jnp.broadcast_to(q[t:t + 1, :], qblk_ref.shape), qrep)
    qblk_ref[...] = jnp.where(c // HEAD_DIM == r // n_q, qrep, 0.0)
    m_ref[...] = jnp.full_like(m_ref, NEG_BIG)
    l_ref[...] = jnp.zeros_like(l_ref)
    acc_ref[...] = jnp.zeros_like(acc_ref)
    kn, vn = kn_ref[0], vn_ref[0]
    s = lax.dot_general(qblk_ref[...], kn, (((1,), (1,)), ((), ())), preferred_element_type=F32)
    rn = lax.broadcasted_iota(jnp.int32, s.shape, 0)
    cn = lax.broadcasted_iota(jnp.int32, s.shape, 1)
    s = jnp.where(cn <= rn % n_q, s, NEG_BIG)
    update(s, lambda p, h: jnp.dot(p, vn[:, V_DIM * h:V_DIM * (h + 1)], preferred_element_type=F32))

    qblk = qblk_ref[...]
    s = jnp.concatenate([jnp.dot(qblk, kt_ref[0], preferred_element_type=F32) for kt_ref in kt_refs], axis=1)

    def head_pv(p, h):
        out = None
        for g, v_ref in enumerate(v_refs):
            part = jnp.dot(p[:, page * g:page * (g + 1)], v_ref[0, pl.ds(h, page, stride=N_HEADS), :],
                           preferred_element_type=F32)
            out = part if out is None else out + part
        return out

    update(s, head_pv)

    o = acc_ref[...] / l_ref[...]
    lam = _diff_lambda(lq1_ref[...], lk1_ref[...], lq2_ref[...], lk2_ref[...], lambda_init)
    for h in range(N_HEADS):
        blk = o[grp * h:grp * (h + 1)]
        o_ref[0, n_q * h:n_q * (h + 1), :] = _diff_norm(blk[0:n_q], blk[n_q:grp], lam, sub_ref[...], lambda_init)


def _attn_body(pt_ref, qt_ref, k_ref, vt_ref, q_ref, kn_ref, vn_ref, *rest, n_pages, lambda_init):
    kt_refs, v_refs = rest[:n_pages], rest[n_pages:2 * n_pages]
    lam_refs = rest[2 * n_pages:2 * n_pages + 5]
    o_prompt_ref, o_sample_ref, w_ref, acc_ref, m_ref, qblk_ref, pm_ref, pl_ref, pacc_ref = rest[2 * n_pages + 5:]
    _flash_body(qt_ref, k_ref, vt_ref, *lam_refs, o_prompt_ref, w_ref, acc_ref, m_ref, lambda_init=lambda_init)
    _paged_body(q_ref, kn_ref, vn_ref, kt_refs, v_refs, lam_refs, o_sample_ref, qblk_ref, pm_ref, pl_ref, pacc_ref,
                lambda_init)


def _attention(qt, kb, vt, q, k_new, v_new, cache_k, cache_v, page_table, lq1, lk1, lq2, lk2, subln, lambda_init):
    B, S, W = kb.shape
    n_chunks, chunk = qt.shape[2], qt.shape[4]
    hp = ATTN_HEADS_PER_STEP
    n_groups = N_HEADS // hp
    Bd, n_q, _ = q.shape
    n_pool, page = cache_k.shape[0], cache_k.shape[1]
    n_pages = page_table.shape[1]
    assert B * n_groups * n_chunks == Bd, "one sample sequence per prompt-attention grid step"
    rows = 2 * N_HEADS * n_q
    ckt = cache_k.transpose(0, 2, 3, 1).reshape(n_pool, W, page)
    cv = cache_v.reshape(n_pool, page * N_HEADS, V_DIM)
    pt = page_table.reshape(-1)

    seq = lambda b, h, i: (b * n_groups + h) * n_chunks + i
    held = pl.Buffered(1)
    vec = lambda n: pl.BlockSpec((1, n), lambda b, h, i, pt: (0, 0))
    per_seq = lambda a: pl.BlockSpec((1,) + a.shape[1:], lambda b, h, i, pt: (seq(b, h, i), 0, 0))

    def page_spec(shape, g):
        return pl.BlockSpec((1,) + shape[1:], lambda b, h, i, pt: (pt[seq(b, h, i) * n_pages + g], 0, 0))

    row = lambda a: a.reshape(1, -1)
    return pl.pallas_call(
        functools.partial(_attn_body, n_pages=n_pages, lambda_init=lambda_init),
        out_shape=(jax.ShapeDtypeStruct((B, S, W), BF16), jax.ShapeDtypeStruct((Bd, N_HEADS * n_q, V_DIM), F32)),
        grid_spec=pltpu.PrefetchScalarGridSpec(
            num_scalar_prefetch=1,
            grid=(B, n_groups, n_chunks),
            in_specs=[
                pl.BlockSpec((1, hp, 1, V_DIM, chunk), lambda b, h, i, pt: (b, h, i, 0, 0)),
                pl.BlockSpec((1, S, hp * V_DIM), lambda b, h, i, pt: (b, 0, h), pipeline_mode=held),
                pl.BlockSpec((1, hp, n_chunks, vt.shape[3], chunk), lambda b, h, i, pt: (b, h, 0, 0, 0),
                             pipeline_mode=held),
                per_seq(q), per_seq(k_new), per_seq(v_new)]
                + [page_spec(ckt.shape, g) for g in range(n_pages)]
                + [page_spec(cv.shape, g) for g in range(n_pages)]
                + [vec(HEAD_DIM)] * 4 + [vec(V_DIM)],
            out_specs=(pl.BlockSpec((1, chunk, hp * V_DIM), lambda b, h, i, pt: (b, i, h)),
                       pl.BlockSpec((1, N_HEADS * n_q, V_DIM), lambda b, h, i, pt: (seq(b, h, i), 0, 0))),
            scratch_shapes=[pltpu.VMEM((hp, V_DIM, 2 * chunk), BF16), pltpu.VMEM((hp, vt.shape[3], 2 * chunk), F32),
                            pltpu.VMEM((hp, 1, 2 * chunk), F32),
                            pltpu.VMEM((rows, W), F32), pltpu.VMEM((rows, 1), F32),
                            pltpu.VMEM((rows, 1), F32), pltpu.VMEM((rows, V_DIM), F32)],
        ),
        compiler_params=_cparams(("arbitrary", "arbitrary", "arbitrary")),
        name="diff_attention",
    )(pt, qt, kb, vt, q, k_new, v_new, *([ckt] * n_pages), *([cv] * n_pages),
      row(lq1), row(lk1), row(lq2), row(lk2), row(subln))


def _rope_tables(pos):
    half = HEAD_DIM // 2
    lane = jnp.arange(LANES)
    inv = jnp.power(ROPE_THETA, -(lane % half).astype(F32) * 2.0 / HEAD_DIM)
    ang = pos.astype(F32)[:, None] * inv[None, :]
    cos, sin = jnp.cos(ang), jnp.sin(ang)
    upper = (lane % HEAD_DIM) >= half
    return cos, jnp.where(upper, sin, 0.0), jnp.where(upper, 0.0, -sin)


def kernel(x_prompt, x_sample, p_prompt, p_sample, cache_k, cache_v, page_table, state_conv, state_rglru, ffn1_norm, ffn1_w_gu, ffn1_w_down, mix_norm, rg_w_in, rg_conv_w, rg_conv_b, rg_w_a, rg_b_a, rg_w_i, rg_b_i, rg_lambda, rg_w_out, kv_norm, w_kv, attn_w_q, lambda_q1, lambda_k1, lambda_q2, lambda_k2, attn_subln, attn_w_o, ffn2_norm, ffn2_w_gu, ffn2_w_down, ple_norm, ple_w_gate, ple_w_proj, final_norm):
    B, S, D = x_prompt.shape
    Bd, Sd, _ = x_sample.shape
    depth = ffn1_norm.shape[0]
    n_a = rg_w_in.shape[0]
    assert depth == 2 and n_a == 1, "one recurrent layer followed by one attention layer"
    assert D == N_HEADS * V_DIM
    n_pages, page = page_table.shape[1], cache_k.shape[1]
    past_len = n_pages * page
    kd = 2 * N_HEADS * HEAD_DIM
    lambda_init = 0.8 - 0.6 * math.exp(-0.3 * 1)

    bf = lambda w: w.astype(BF16)
    w_gu1, w_dn1, w_gu2, w_dn2 = bf(ffn1_w_gu), bf(ffn1_w_down), bf(ffn2_w_gu), bf(ffn2_w_down)
    w_in, w_out, w_kvb, w_q, w_o = bf(rg_w_in[0]), bf(rg_w_out[0]), bf(w_kv), bf(attn_w_q[0]), bf(attn_w_o[0])
    w_pg, w_pp = bf(ple_w_gate), bf(ple_w_proj)
    wai = bf(jnp.concatenate([rg_w_a[0], rg_w_i[0]], axis=-1))

    def layer0_pre(x):
        return _ffn_block(x, ffn1_norm[0], w_gu1, w_dn1, 0, ("rg_in", mix_norm[0], w_in))

    def layer0_post(x, p_all, tabs, seq_shape):
        x = _ffn_block(x, ffn2_norm[0], w_gu2, w_dn2, 0, ("ple", p_all, ple_norm[0], w_pg, w_pp, final_norm))
        kv = _kv_proj(x, kv_norm, w_kvb, tabs, kd, seq_shape)
        x, q = _ffn_block(x, ffn1_norm[1], w_gu1, w_dn1, 1, ("q", mix_norm[1], w_q, tabs, seq_shape))
        return x, q, kv

    def layer1_post(x, dn, p_all):
        return _ffn_block(x, ffn2_norm[1], w_gu2, w_dn2, 1, ("ple", p_all, ple_norm[1], w_pg, w_pp, final_norm),
                          attn_out=(dn, w_o), final=True)

    rg_args = (rg_conv_w[0], rg_conv_b[0], wai, rg_b_a[0], rg_b_i[0], rg_lambda[0], w_out)
    lam_args = (lambda_q1[0], lambda_k1[0], lambda_q2[0], lambda_k2[0], attn_subln[0], lambda_init)

    Tp = B * S
    tabs_p = _rope_tables(jnp.arange(S))
    p_p = p_prompt.reshape(depth, Tp, -1)
    x, gate, u = layer0_pre(x_prompt.reshape(Tp, D))
    x, new_conv_p, new_h_p = _rg_prompt(gate.reshape(B, S, D), u.reshape(B, S, D), x.reshape(B, S, D), *rg_args)
    x_p, qt, (k_p, v_p, kb, vt) = layer0_post(x.reshape(Tp, D), p_p, tabs_p, (B, S))

    Ts = Sd * Bd
    tm = lambda a: jnp.swapaxes(a, 0, 1)
    tabs_s = _rope_tables(jnp.repeat(past_len + jnp.arange(Sd), Bd))
    p_s = jnp.swapaxes(p_sample, 1, 2).reshape(depth, Ts, -1)
    x, gate, u = layer0_pre(tm(x_sample).reshape(Ts, D))
    x, new_conv_s, new_h_s = _rg_sample(gate.reshape(Sd, Bd, D), u.reshape(Sd, Bd, D), x.reshape(Sd, Bd, D),
                                        tm(state_conv[0]), state_rglru[0], *rg_args)
    x_s, q, (k_s, v_s) = layer0_post(x.reshape(Ts, D), p_s, tabs_s, None)
    bm = lambda a: tm(a.reshape(Sd, Bd, -1))
    k_s, v_s = bm(k_s), bm(v_s)
    pad = lambda a: jnp.pad(a, ((0, 0), (0, -Sd % SUBLANES), (0, 0)))

    dn_p, dn_s = _attention(qt, kb.reshape(B, S, D), vt, bm(q), pad(k_s), pad(v_s), cache_k, cache_v, page_table,
                            *lam_args)
    y_prompt = layer1_post(x_p, dn_p.reshape(Tp, D), p_p).reshape(B, S, D)
    dn_s = dn_s.reshape(Bd, N_HEADS, Sd, V_DIM).transpose(2, 0, 1, 3).reshape(Ts, D)
    y_sample = bm(layer1_post(x_s, dn_s, p_s))

    return (y_prompt, y_sample,
            k_p.reshape(B, 2 * N_HEADS, HEAD_DIM, S).transpose(0, 3, 1, 2), v_p.reshape(B, S, N_HEADS, V_DIM),
            k_s.reshape(Bd, Sd, 2 * N_HEADS, HEAD_DIM), v_s.reshape(Bd, Sd, N_HEADS, V_DIM),
            new_conv_p[None], new_h_p.reshape(1, B, D),
            tm(new_conv_s)[None], new_h_s[None])
```

```python
import functools
import math

import jax
import jax.numpy as jnp
from jax import lax
from jax.experimental import pallas as pl
from jax.experimental.pallas import tpu as pltpu

F32 = jnp.float32
BF16 = jnp.bfloat16

EPS = 1e-6
N_HEADS = 8
HEAD_DIM = 64
V_DIM = 2 * HEAD_DIM
N_RG_BLOCKS = 8
CONV_W = 4
RG_C = 8.0
ROPE_THETA = 10000.0
LANES = 128
SUBLANES = 8
NEG_BIG = -1e30
LOG2_E = math.log2(math.e)
ONES_ROWS = 16
VMEM_LIMIT = 56 * 1024 * 1024

TOKEN_TILE = 512
SEQ_TILE = 512
ATTN_CHUNK = 256
ATTN_JOINT = 4
ATTN_HEADS_PER_STEP = 4


def _cparams(sem):
    return pltpu.CompilerParams(dimension_semantics=sem, vmem_limit_bytes=VMEM_LIMIT)


def _rms(x, g):
    return x * lax.rsqrt(jnp.mean(x * x, axis=-1, keepdims=True) + EPS) * g


def _rope(x, cos, sin_hi, sin_lo):
    outs = []
    for j in range(x.shape[1] // LANES):
        xj = x[:, j * LANES:(j + 1) * LANES]
        outs.append(xj * cos
                    + pltpu.roll(xj, LANES - HEAD_DIM // 2, 1) * sin_lo
                    + pltpu.roll(xj, HEAD_DIM // 2, 1) * sin_hi)
    return jnp.concatenate(outs, axis=1)


def _gelu_tanh(x):
    return 0.5 * x * (1.0 + jnp.tanh(math.sqrt(2.0 / math.pi) * (x + 0.044715 * (x * x * x))))


def _softplus(z):
    return jnp.maximum(z, 0.0) + jnp.log1p(jnp.exp(-jnp.abs(z)))


def _ffn_split(d_ff):
    for n_split in (2, 4, 1):
        if d_ff % (n_split * LANES) == 0:
            return n_split
    raise ValueError(f"unsupported FFN width {d_ff}")


def _store_head_transposed(dst_ref, x):
    n_chunks, n_rows, chunk = dst_ref.shape[2], dst_ref.shape[3], dst_ref.shape[4]
    for h in range(N_HEADS):
        xt = x[:, h * V_DIM:(h + 1) * V_DIM].T.astype(dst_ref.dtype)
        for c in range(n_chunks):
            dst_ref[0, h, c, 0:V_DIM, :] = xt[:, c * chunk:(c + 1) * chunk]
            if n_rows > V_DIM:
                dst_ref[0, h, c, V_DIM:n_rows, :] = jnp.ones((n_rows - V_DIM, chunk), dst_ref.dtype)


def _head_transposed_out(B, S, tm, dtype, extra_rows=0):
    tiles_per_seq = S // tm
    rows = V_DIM + extra_rows
    shape = jax.ShapeDtypeStruct((B, N_HEADS, S // ATTN_CHUNK, rows, ATTN_CHUNK), dtype)
    spec = pl.BlockSpec((1, N_HEADS, tm // ATTN_CHUNK, rows, ATTN_CHUNK),
                        lambda i: (i // tiles_per_seq, 0, i % tiles_per_seq, 0, 0))
    return shape, spec


def _table_specs(tm, n_tab_tiles):
    return [pl.BlockSpec((tm, LANES), lambda i: (i % n_tab_tiles, 0))] * 3


def _kv_store(x, g_ref, w_ref, cos_ref, shi_ref, slo_ref, k_ref, v_ref, *lowp_refs):
    n = _rms(x, g_ref[...]).astype(BF16)
    kv = jnp.dot(n, w_ref[...], preferred_element_type=F32)
    kd = N_HEADS * V_DIM
    k = _rope(kv[:, :kd], cos_ref[...], shi_ref[...], slo_ref[...])
    v = kv[:, kd:]
    if not lowp_refs:
        k_ref[...] = k
        v_ref[...] = v
    else:
        for h in range(N_HEADS):
            k_ref[0, h] = k[:, h * V_DIM:(h + 1) * V_DIM].T
        v_ref[...] = v
        kb_ref, vt_ref = lowp_refs
        kb_ref[...] = k.astype(kb_ref.dtype)
        _store_head_transposed(vt_ref, v)


def _kv_outputs(T, tm, kd, vd, seq_shape):
    out_shape = [jax.ShapeDtypeStruct((T, kd), F32), jax.ShapeDtypeStruct((T, vd), F32)]
    out_specs = [pl.BlockSpec((tm, kd), lambda i: (i, 0)), pl.BlockSpec((tm, vd), lambda i: (i, 0))]
    if seq_shape is not None:
        B, S = seq_shape
        tiles_per_seq = S // tm
        out_shape[0] = jax.ShapeDtypeStruct((B, N_HEADS, V_DIM, S), F32)
        out_specs[0] = pl.BlockSpec((1, N_HEADS, V_DIM, tm), lambda i: (i // tiles_per_seq, 0, 0, i % tiles_per_seq))
        vt_shape, vt_spec = _head_transposed_out(B, S, tm, BF16, extra_rows=ONES_ROWS)
        out_shape += [jax.ShapeDtypeStruct((T, kd), BF16), vt_shape]
        out_specs += [pl.BlockSpec((tm, kd), lambda i: (i, 0)), vt_spec]
    return out_shape, out_specs


def _block_body(*refs, pre, post, n_split, transposed_q, final, with_kv):
    refs = list(refs)
    take = lambda k: [refs.pop(0) for _ in range(k)]
    (x_ref,) = take(1)
    x = x_ref[...]
    if pre == "attn_out":
        dn_ref, wo_ref = take(2)
        x = x + jnp.dot(dn_ref[...].astype(BF16), wo_ref[...], preferred_element_type=F32)
    g_ref, wgu_ref, wd_ref = take(3)

    n = _rms(x, g_ref[...]).astype(BF16)
    d_ff = wd_ref.shape[0]
    tf = d_ff // n_split
    y = None
    for c in range(n_split):
        g = jnp.dot(n, wgu_ref[:, c * tf:(c + 1) * tf], preferred_element_type=F32)
        u = jnp.dot(n, wgu_ref[:, d_ff + c * tf:d_ff + (c + 1) * tf], preferred_element_type=F32)
        h = (g * jax.nn.sigmoid(g) * u).astype(BF16)
        part = jnp.dot(h, wd_ref[c * tf:(c + 1) * tf, :], preferred_element_type=F32)
        y = part if y is None else y + part
    x = x + 0.5 * y

    if post == "rg_in":
        mg_ref, win_ref, x_out, gate_ref, u_ref = take(5)
        x_out[...] = x
        z = jnp.dot(_rms(x, mg_ref[...]).astype(BF16), win_ref[...], preferred_element_type=F32)
        d = gate_ref.shape[1]
        gate_ref[...] = z[:, :d]
        u_ref[...] = z[:, d:]
    elif post == "q":
        mg_ref, wq_ref, cos_ref, shi_ref, slo_ref, x_out, q_ref = take(7)
        x_out[...] = x
        q = jnp.dot(_rms(x, mg_ref[...]).astype(BF16), wq_ref[...], preferred_element_type=F32)
        q = _rope(q, cos_ref[...], shi_ref[...], slo_ref[...]) * (HEAD_DIM ** -0.5)
        if transposed_q:
            _store_head_transposed(q_ref, q * LOG2_E)
        else:
            q_ref[...] = q
    else:
        p_ref, pg_ref, wpg_ref, wpp_ref, fin_ref = take(5)
        kv_in = take(5) if with_kv else None
        (x_out,) = take(1)
        gate = jax.nn.sigmoid(jnp.dot(_rms(x, pg_ref[...]).astype(BF16), wpg_ref[...], preferred_element_type=F32))
        x = x + gate * jnp.dot(p_ref[...].astype(BF16), wpp_ref[...], preferred_element_type=F32)
        x_out[...] = _rms(x, fin_ref[...]) if final else x
        if with_kv:
            _kv_store(x, *kv_in, *refs)


def _ffn_block(x, g, w_gu_all, w_down_all, layer, post, *, attn_out=None, final=False):
    T, D = x.shape
    d_ff = w_down_all.shape[1]
    tm = min(TOKEN_TILE, T)
    resident = pl.Buffered(1)
    tok = lambda width: pl.BlockSpec((tm, width), lambda i: (i, 0))
    vec = pl.BlockSpec((1, D), lambda i: (0, 0))
    whole = lambda a: pl.BlockSpec(a.shape, lambda i: (0,) * a.ndim, pipeline_mode=resident)
    layer_of = lambda a: pl.BlockSpec((None,) + a.shape[1:], lambda i: (layer, 0, 0), pipeline_mode=resident)
    row = lambda a: a.reshape(1, D)

    args, in_specs = [x], [tok(D)]
    if attn_out is not None:
        dn, w_o = attn_out
        args += [dn, w_o]
        in_specs += [tok(dn.shape[1]), whole(w_o)]
    args += [row(g), w_gu_all, w_down_all]
    in_specs += [vec, layer_of(w_gu_all), layer_of(w_down_all)]

    kind = post[0]
    transposed_q = False
    if kind == "rg_in":
        _, mix_g, w_in = post
        N = w_in.shape[1] // 2
        args += [row(mix_g), w_in]
        in_specs += [vec, whole(w_in)]
        out_shape = (jax.ShapeDtypeStruct((T, D), F32),) + (jax.ShapeDtypeStruct((T, N), F32),) * 2
        out_specs = (tok(D), tok(N), tok(N))
    elif kind == "q":
        _, mix_g, w_q, tabs, seq_shape = post
        args += [row(mix_g), w_q, *tabs]
        in_specs += [vec, whole(w_q)] + _table_specs(tm, tabs[0].shape[0] // tm)
        transposed_q = seq_shape is not None
        if transposed_q:
            q_shape, q_spec = _head_transposed_out(*seq_shape, tm, BF16)
        else:
            q_shape, q_spec = jax.ShapeDtypeStruct((T, w_q.shape[1]), F32), tok(w_q.shape[1])
        out_shape = (jax.ShapeDtypeStruct((T, D), F32), q_shape)
        out_specs = (tok(D), q_spec)
    else:
        _, p_all, ple_g, w_gate_all, w_proj_all, fin_g, kv = post
        P = p_all.shape[2]
        args += [p_all, row(ple_g), w_gate_all, w_proj_all, row(fin_g)]
        in_specs += [pl.BlockSpec((None, tm, P), lambda i: (layer, i, 0)), vec,
                     layer_of(w_gate_all), layer_of(w_proj_all), vec]
        out_shape = jax.ShapeDtypeStruct((T, D), F32)
        out_specs = tok(D)
        if kv is not None:
            kv_g, w_kv, tabs, kd, seq_shape = kv
            args += [row(kv_g), w_kv, *tabs]
            in_specs += [vec, whole(w_kv)] + _table_specs(tm, tabs[0].shape[0] // tm)
            kv_shapes, kv_specs = _kv_outputs(T, tm, kd, w_kv.shape[1] - kd, seq_shape)
            out_shape = (out_shape, *kv_shapes)
            out_specs = (out_specs, *kv_specs)
    return pl.pallas_call(
        functools.partial(_block_body, pre=None if attn_out is None else "attn_out", post=kind,
                          n_split=_ffn_split(d_ff), transposed_q=transposed_q, final=final,
                          with_kv=kind == "ple" and post[-1] is not None),
        out_shape=out_shape,
        grid=(T // tm,),
        in_specs=in_specs,
        out_specs=out_specs,
        compiler_params=_cparams(("parallel",)),
        name="ffn_" + kind,
    )(*args)


def _rg_coeffs(conv, wai_ref, ba, bi, sp):
    cb = conv.astype(BF16)
    nb = wai_ref.shape[0]
    blk = conv.shape[1] // nb
    a_parts, b_parts = [], []
    for n in range(nb):
        sl = slice(n * blk, (n + 1) * blk)
        g = jnp.dot(cb[:, sl], wai_ref[n], preferred_element_type=F32)
        r = jax.nn.sigmoid(g[:, :blk] + ba[:, sl])
        i = jax.nn.sigmoid(g[:, blk:] + bi[:, sl])
        log_a = -RG_C * r * sp[:, sl]
        a = jnp.exp(log_a)
        mult = jnp.sqrt(-jnp.tanh(log_a) * (a * a + 1.0))
        a_parts.append(a)
        b_parts.append(mult * (i * conv[:, sl]))
    return a_parts, b_parts


def _rg_prompt_body(gate_ref, u_ref, x_ref, cw_ref, cb_ref, wai_ref, ba_ref, bi_ref, lam_ref, wo_ref,
                    o_ref, nconv_ref, nh_ref, ubuf_ref, a_ref, h_ref, hstate_ref):
    s = pl.program_id(1)
    ts, D = u_ref.shape[1], u_ref.shape[2]

    @pl.when(s == 0)
    def _():
        ubuf_ref[0:SUBLANES, :] = jnp.zeros((SUBLANES, D), F32)
        hstate_ref[...] = jnp.zeros_like(hstate_ref)

    u = u_ref[0]
    ubuf_ref[SUBLANES:SUBLANES + ts, :] = u
    cw = cw_ref[...]
    conv = cb_ref[...] + u * cw[CONV_W - 1:CONV_W, :]
    for k in range(1, CONV_W):
        conv = conv + ubuf_ref[SUBLANES - k:SUBLANES - k + ts, :] * cw[CONV_W - 1 - k:CONV_W - k, :]

    sp = _softplus(-lam_ref[...])
    a, b = _rg_coeffs(conv, wai_ref, ba_ref[...], bi_ref[...], sp)
    a_ref[...] = jnp.concatenate(a, axis=1)
    h_ref[...] = jnp.concatenate(b, axis=1)

    row = lax.broadcasted_iota(jnp.int32, (SUBLANES, D), 0)

    def group(gi, h_prev):
        r0 = pl.multiple_of(gi * SUBLANES, SUBLANES)
        ag = a_ref[pl.ds(r0, SUBLANES), :]
        bg = h_ref[pl.ds(r0, SUBLANES), :]
        for d in (1, 2, 4):
            m = row >= d
            a_sh = pltpu.roll(ag, d, 0)
            b_sh = pltpu.roll(bg, d, 0)
            bg = jnp.where(m, ag * b_sh + bg, bg)
            ag = jnp.where(m, ag * a_sh, ag)
        hg = ag * h_prev + bg
        h_ref[pl.ds(r0, SUBLANES), :] = hg
        return jnp.broadcast_to(hg[SUBLANES - 1:SUBLANES, :], (SUBLANES, D))

    h_last = lax.fori_loop(0, ts // SUBLANES, group, hstate_ref[...])
    hstate_ref[...] = h_last
    ubuf_ref[0:SUBLANES, :] = u[ts - SUBLANES:ts, :]

    y = (_gelu_tanh(gate_ref[0]) * h_ref[...]).astype(BF16)
    o_ref[0] = x_ref[0] + jnp.dot(y, wo_ref[...], preferred_element_type=F32)

    @pl.when(s == pl.num_programs(1) - 1)
    def _():
        nconv_ref[0] = u[ts - (CONV_W - 1):ts, :]
        nh_ref[0] = h_last[0:1, :]


def _rg_prompt(gate, u, x, cw, cb, wai, ba, bi, lam, wo):
    B, S, D = u.shape
    ts = min(SEQ_TILE, S)
    row = lambda a: a.reshape(1, D)
    tok = pl.BlockSpec((1, ts, D), lambda b, s: (b, s, 0))
    vec = pl.BlockSpec((1, D), lambda b, s: (0, 0))
    return pl.pallas_call(
        _rg_prompt_body,
        out_shape=(jax.ShapeDtypeStruct((B, S, D), F32),
                   jax.ShapeDtypeStruct((B, CONV_W - 1, D), F32),
                   jax.ShapeDtypeStruct((B, 1, D), F32)),
        grid=(B, S // ts),
        in_specs=[tok, tok, tok,
                  pl.BlockSpec((CONV_W, D), lambda b, s: (0, 0)), vec,
                  pl.BlockSpec(wai.shape, lambda b, s: (0, 0, 0)), vec, vec, vec,
                  pl.BlockSpec((D, D), lambda b, s: (0, 0))],
        out_specs=(tok,
                   pl.BlockSpec((1, CONV_W - 1, D), lambda b, s: (b, 0, 0)),
                   pl.BlockSpec((1, 1, D), lambda b, s: (b, 0, 0))),
        scratch_shapes=[pltpu.VMEM((ts + SUBLANES, D), F32), pltpu.VMEM((ts, D), F32),
                        pltpu.VMEM((ts, D), F32), pltpu.VMEM((SUBLANES, D), F32)],
        compiler_params=_cparams(("parallel", "arbitrary")),
        name="rg_prompt",
    )(gate, u, x, cw, row(cb), wai, row(ba), row(bi), row(lam), wo)


def _rg_sample_body(gate_ref, u_ref, x_ref, cprev_ref, h0_ref, cw_ref, cb_ref, wai_ref, ba_ref, bi_ref,
                    lam_ref, wo_ref, o_ref, nconv_ref, nh_ref):
    n_t = u_ref.shape[0]
    cw = cw_ref[...]
    hist = [cprev_ref[k] for k in range(CONV_W - 1)] + [u_ref[t] for t in range(n_t)]
    sp = _softplus(-lam_ref[...])
    h = h0_ref[...]
    for t in range(n_t):
        conv = cb_ref[...] + hist[t] * cw[0:1, :]
        for k in range(1, CONV_W):
            conv = conv + hist[t + k] * cw[k:k + 1, :]
        a, b = _rg_coeffs(conv, wai_ref, ba_ref[...], bi_ref[...], sp)
        h = jnp.concatenate(a, axis=1) * h + jnp.concatenate(b, axis=1)
        y = (_gelu_tanh(gate_ref[t]) * h).astype(BF16)
        o_ref[t] = x_ref[t] + jnp.dot(y, wo_ref[...], preferred_element_type=F32)
    for k in range(CONV_W - 1):
        nconv_ref[k] = hist[n_t + k]
    nh_ref[...] = h


def _rg_sample(gate, u, x, cprev, h0, cw, cb, wai, ba, bi, lam, wo):
    n_t, Bd, D = u.shape
    row = lambda a: a.reshape(1, D)
    return pl.pallas_call(
        _rg_sample_body,
        out_shape=(jax.ShapeDtypeStruct((n_t, Bd, D), F32),
                   jax.ShapeDtypeStruct((CONV_W - 1, Bd, D), F32),
                   jax.ShapeDtypeStruct((Bd, D), F32)),
        compiler_params=pltpu.CompilerParams(vmem_limit_bytes=VMEM_LIMIT),
        name="rg_sample",
    )(gate, u, x, cprev, h0, cw, row(cb), wai, row(ba), row(bi), row(lam), wo)


def _diff_lambda(lq1, lk1, lq2, lk2, lambda_init):
    return (jnp.exp(jnp.sum(lq1 * lk1, axis=-1, keepdims=True))
            - jnp.exp(jnp.sum(lq2 * lk2, axis=-1, keepdims=True)) + lambda_init)


def _diff_norm(o0, o1, lam, subln, lambda_init):
    d = o0 - lam * o1
    return _rms(d, subln) * (1.0 - lambda_init)


def _flash_body(qt_ref, k_ref, vt_ref, lq1_ref, lk1_ref, lq2_ref, lk2_ref, sub_ref, o_ref, w_ref, acc_ref,
                m_ref, *, lambda_init):
    i = pl.program_id(2)
    n_heads = qt_ref.shape[1]
    tq = qt_ref.shape[4]
    chunk = vt_ref.shape[4]

    for hh in range(n_heads):
        qt = qt_ref[0, hh, 0]
        feat = lax.broadcasted_iota(jnp.int32, qt.shape, 0)
        zero = jnp.zeros_like(qt)
        w_ref[hh, :, 0:tq] = jnp.where(feat < HEAD_DIM, qt, zero)
        w_ref[hh, :, tq:2 * tq] = jnp.where(feat < HEAD_DIM, zero, qt)
    acc_ref[...] = jnp.zeros_like(acc_ref)
    m_ref[...] = jnp.full_like(m_ref, NEG_BIG)

    def attend(hh, first, n_full, diagonal):
        cs = [first + t for t in range(n_full)] + ([i] if diagonal else [])
        s = []
        for t, c in enumerate(cs):
            r0 = pl.multiple_of(c * chunk, chunk)
            st = jnp.dot(k_ref[0, pl.ds(r0, chunk), V_DIM * hh:V_DIM * (hh + 1)], w_ref[hh],
                         preferred_element_type=F32)
            if diagonal and t == n_full:
                key = lax.broadcasted_iota(jnp.int32, st.shape, 0)
                qry = lax.broadcasted_iota(jnp.int32, st.shape, 1)
                qry = jnp.where(qry >= tq, qry - tq, qry)
                st = jnp.where(key <= qry, st, NEG_BIG)
            s.append(st)
        m_prev = m_ref[hh]
        m_new = m_prev
        for st in s:
            m_new = jnp.maximum(m_new, jnp.max(st, axis=0, keepdims=True))
        alpha = jnp.exp2(m_prev - m_new)
        pv = None
        for st, c in zip(s, cs):
            p = jnp.exp2(st - m_new).astype(BF16)
            part = jnp.dot(vt_ref[0, hh, c], p, preferred_element_type=F32)
            pv = part if pv is None else pv + part
        acc_ref[hh] = alpha * acc_ref[hh] + pv
        m_ref[hh] = m_new

    def main(g, carry):
        for hh in range(n_heads):
            attend(hh, g * ATTN_JOINT, ATTN_JOINT, False)
        return carry

    lax.fori_loop(0, i // ATTN_JOINT, main, 0)
    for r in range(ATTN_JOINT):
        @pl.when(i % ATTN_JOINT == r)
        def _():
            for hh in range(n_heads):
                attend(hh, i - r, r, True)

    lam = _diff_lambda(lq1_ref[...], lk1_ref[...], lq2_ref[...], lk2_ref[...], lambda_init)
    for hh in range(n_heads):
        acc = acc_ref[hh]
        o = acc[0:V_DIM] / acc[V_DIM:V_DIM + 1]
        d = (o[:, 0:tq] - lam * o[:, tq:2 * tq]).T
        o_ref[0, :, V_DIM * hh:V_DIM * (hh + 1)] = (_rms(d, sub_ref[...]) * (1.0 - lambda_init)).astype(o_ref.dtype)


def _paged_body(q_ref, kn_ref, vn_ref, kt_refs, v_refs, lam_refs, o_ref, qblk_ref, m_ref, l_ref, acc_ref,
                lambda_init):
    lq1_ref, lk1_ref, lq2_ref, lk2_ref, sub_ref = lam_refs
    n_q = q_ref.shape[1]
    page = kt_refs[0].shape[2]
    grp = 2 * n_q

    def update(s, head_pv):
        m_prev = m_ref[...]
        m_new = jnp.maximum(m_prev, jnp.max(s, axis=1, keepdims=True))
        alpha = jnp.exp(m_prev - m_new)
        p = jnp.exp(s - m_new)
        l_ref[...] = alpha * l_ref[...] + jnp.sum(p, axis=1, keepdims=True)
        m_ref[...] = m_new
        for h in range(N_HEADS):
            sl = slice(grp * h, grp * (h + 1))
            acc_ref[sl, :] = alpha[sl] * acc_ref[sl, :] + head_pv(p[sl], h)

    q = q_ref[0]
    r = lax.broadcasted_iota(jnp.int32, qblk_ref.shape, 0)
    c = lax.broadcasted_iota(jnp.int32, qblk_ref.shape, 1)
    qrep = jnp.zeros(qblk_ref.shape, F32)
    for t in range(n_q):
        qrep = jnp.where(r % n_q == t, jnp.broadcast_to(q[t:t + 1, :], qblk_ref.shape), qrep)
    qblk_ref[...] = jnp.where(c // HEAD_DIM == r // n_q, qrep, 0.0)
    m_ref[...] = jnp.full_like(m_ref, NEG_BIG)
    l_ref[...] = jnp.zeros_like(l_ref)
    acc_ref[...] = jnp.zeros_like(acc_ref)
    kn, vn = kn_ref[0], vn_ref[0]
    s = lax.dot_general(qblk_ref[...], kn, (((1,), (1,)), ((), ())), preferred_element_type=F32)
    rn = lax.broadcasted_iota(jnp.int32, s.shape, 0)
    cn = lax.broadcasted_iota(jnp.int32, s.shape, 1)
    s = jnp.where(cn <= rn % n_q, s, NEG_BIG)
    update(s, lambda p, h: jnp.dot(p, vn[:, V_DIM * h:V_DIM * (h + 1)], preferred_element_type=F32))

    qblk = qblk_ref[...]
    s = jnp.concatenate([jnp.dot(qblk, kt_ref[0], preferred_element_type=F32) for kt_ref in kt_refs], axis=1)

    def head_pv(p, h):
        out = None
        for g, v_ref in enumerate(v_refs):
            part = jnp.dot(p[:, page * g:page * (g + 1)], v_ref[0, pl.ds(h, page, stride=N_HEADS), :],
                           preferred_element_type=F32)
            out = part if out is None else out + part
        return out

    update(s, head_pv)

    o = acc_ref[...] / l_ref[...]
    lam = _diff_lambda(lq1_ref[...], lk1_ref[...], lq2_ref[...], lk2_ref[...], lambda_init)
    for h in range(N_HEADS):
        blk = o[grp * h:grp * (h + 1)]
        o_ref[0, n_q * h:n_q * (h + 1), :] = _diff_norm(blk[0:n_q], blk[n_q:grp], lam, sub_ref[...], lambda_init)


def _attn_body(pt_ref, qt_ref, k_ref, vt_ref, q_ref, kn_ref, vn_ref, *rest, n_pages, lambda_init):
    kt_refs, v_refs = rest[:n_pages], rest[n_pages:2 * n_pages]
    lam_refs = rest[2 * n_pages:2 * n_pages + 5]
    o_prompt_ref, o_sample_ref, w_ref, acc_ref, m_ref, qblk_ref, pm_ref, pl_ref, pacc_ref = rest[2 * n_pages + 5:]
    _flash_body(qt_ref, k_ref, vt_ref, *lam_refs, o_prompt_ref, w_ref, acc_ref, m_ref, lambda_init=lambda_init)
    _paged_body(q_ref, kn_ref, vn_ref, kt_refs, v_refs, lam_refs, o_sample_ref, qblk_ref, pm_ref, pl_ref, pacc_ref,
                lambda_init)


def _attention(qt, kb, vt, q, k_new, v_new, cache_k, cache_v, page_table, lq1, lk1, lq2, lk2, subln, lambda_init):
    B, S, W = kb.shape
    n_chunks, chunk = qt.shape[2], qt.shape[4]
    hp = ATTN_HEADS_PER_STEP
    n_groups = N_HEADS // hp
    Bd, n_q, _ = q.shape
    n_pool, page = cache_k.shape[0], cache_k.shape[1]
    n_pages = page_table.shape[1]
    assert B * n_groups * n_chunks == Bd, "one sample sequence per prompt-attention grid step"
    rows = 2 * N_HEADS * n_q
    ckt = cache_k.transpose(0, 2, 3, 1).reshape(n_pool, W, page)
    cv = cache_v.reshape(n_pool, page * N_HEADS, V_DIM)
    pt = page_table.reshape(-1)

    seq = lambda b, h, i: (b * n_groups + h) * n_chunks + i
    held = pl.Buffered(1)
    vec = lambda n: pl.BlockSpec((1, n), lambda b, h, i, pt: (0, 0))
    per_seq = lambda a: pl.BlockSpec((1,) + a.shape[1:], lambda b, h, i, pt: (seq(b, h, i), 0, 0))

    def page_spec(shape, g):
        return pl.BlockSpec((1,) + shape[1:], lambda b, h, i, pt: (pt[seq(b, h, i) * n_pages + g], 0, 0))

    row = lambda a: a.reshape(1, -1)
    return pl.pallas_call(
        functools.partial(_attn_body, n_pages=n_pages, lambda_init=lambda_init),
        out_shape=(jax.ShapeDtypeStruct((B, S, W), BF16), jax.ShapeDtypeStruct((Bd, N_HEADS * n_q, V_DIM), F32)),
        grid_spec=pltpu.PrefetchScalarGridSpec(
            num_scalar_prefetch=1,
            grid=(B, n_groups, n_chunks),
            in_specs=[
                pl.BlockSpec((1, hp, 1, V_DIM, chunk), lambda b, h, i, pt: (b, h, i, 0, 0)),
                pl.BlockSpec((1, S, hp * V_DIM), lambda b, h, i, pt: (b, 0, h), pipeline_mode=held),
                pl.BlockSpec((1, hp, n_chunks, vt.shape[3], chunk), lambda b, h, i, pt: (b, h, 0, 0, 0),
                             pipeline_mode=held),
                per_seq(q), per_seq(k_new), per_seq(v_new)]
                + [page_spec(ckt.shape, g) for g in range(n_pages)]
                + [page_spec(cv.shape, g) for g in range(n_pages)]
                + [vec(HEAD_DIM)] * 4 + [vec(V_DIM)],
            out_specs=(pl.BlockSpec((1, chunk, hp * V_DIM), lambda b, h, i, pt: (b, i, h)),
                       pl.BlockSpec((1, N_HEADS * n_q, V_DIM), lambda b, h, i, pt: (seq(b, h, i), 0, 0))),
            scratch_shapes=[pltpu.VMEM((hp, V_DIM, 2 * chunk), BF16), pltpu.VMEM((hp, vt.shape[3], 2 * chunk), F32),
                            pltpu.VMEM((hp, 1, 2 * chunk), F32),
                            pltpu.VMEM((rows, W), F32), pltpu.VMEM((rows, 1), F32),
                            pltpu.VMEM((rows, 1), F32), pltpu.VMEM((rows, V_DIM), F32)],
        ),
        compiler_params=_cparams(("arbitrary", "arbitrary", "arbitrary")),
        name="diff_attention",
    )(pt, qt, kb, vt, q, k_new, v_new, *([ckt] * n_pages), *([cv] * n_pages),
      row(lq1), row(lk1), row(lq2), row(lk2), row(subln))


def _rope_tables(pos):
    half = HEAD_DIM // 2
    lane = jnp.arange(LANES)
    inv = jnp.power(ROPE_THETA, -(lane % half).astype(F32) * 2.0 / HEAD_DIM)
    ang = pos.astype(F32)[:, None] * inv[None, :]
    cos, sin = jnp.cos(ang), jnp.sin(ang)
    upper = (lane % HEAD_DIM) >= half
    return cos, jnp.where(upper, sin, 0.0), jnp.where(upper, 0.0, -sin)


def kernel(x_prompt, x_sample, p_prompt, p_sample, cache_k, cache_v, page_table, state_conv, state_rglru, ffn1_norm, ffn1_w_gu, ffn1_w_down, mix_norm, rg_w_in, rg_conv_w, rg_conv_b, rg_w_a, rg_b_a, rg_w_i, rg_b_i, rg_lambda, rg_w_out, kv_norm, w_kv, attn_w_q, lambda_q1, lambda_k1, lambda_q2, lambda_k2, attn_subln, attn_w_o, ffn2_norm, ffn2_w_gu, ffn2_w_down, ple_norm, ple_w_gate, ple_w_proj, final_norm):
    B, S, D = x_prompt.shape
    Bd, Sd, _ = x_sample.shape
    depth = ffn1_norm.shape[0]
    n_a = rg_w_in.shape[0]
    assert depth == 2 and n_a == 1, "one recurrent layer followed by one attention layer"
    assert D == N_HEADS * V_DIM
    n_pages, page = page_table.shape[1], cache_k.shape[1]
    past_len = n_pages * page
    kd = 2 * N_HEADS * HEAD_DIM
    lambda_init = 0.8 - 0.6 * math.exp(-0.3 * 1)

    bf = lambda w: w.astype(BF16)
    w_gu1, w_dn1, w_gu2, w_dn2 = bf(ffn1_w_gu), bf(ffn1_w_down), bf(ffn2_w_gu), bf(ffn2_w_down)
    w_in, w_out, w_kvb, w_q, w_o = bf(rg_w_in[0]), bf(rg_w_out[0]), bf(w_kv), bf(attn_w_q[0]), bf(attn_w_o[0])
    w_pg, w_pp = bf(ple_w_gate), bf(ple_w_proj)
    wai = bf(jnp.concatenate([rg_w_a[0], rg_w_i[0]], axis=-1))

    def layer0_pre(x):
        return _ffn_block(x, ffn1_norm[0], w_gu1, w_dn1, 0, ("rg_in", mix_norm[0], w_in))

    def layer0_post(x, p_all, tabs, seq_shape):
        x, *kv = _ffn_block(x, ffn2_norm[0], w_gu2, w_dn2, 0,
                            ("ple", p_all, ple_norm[0], w_pg, w_pp, final_norm, (kv_norm, w_kvb, tabs, kd, seq_shape)))
        x, q = _ffn_block(x, ffn1_norm[1], w_gu1, w_dn1, 1, ("q", mix_norm[1], w_q, tabs, seq_shape))
        return x, q, kv

    def layer1_post(x, dn, p_all):
        return _ffn_block(x, ffn2_norm[1], w_gu2, w_dn2, 1, ("ple", p_all, ple_norm[1], w_pg, w_pp, final_norm, None),
                          attn_out=(dn, w_o), final=True)

    rg_args = (rg_conv_w[0], rg_conv_b[0], wai, rg_b_a[0], rg_b_i[0], rg_lambda[0], w_out)
    lam_args = (lambda_q1[0], lambda_k1[0], lambda_q2[0], lambda_k2[0], attn_subln[0], lambda_init)

    Tp = B * S
    tabs_p = _rope_tables(jnp.arange(S))
    p_p = p_prompt.reshape(depth, Tp, -1)
    x, gate, u = layer0_pre(x_prompt.reshape(Tp, D))
    x, new_conv_p, new_h_p = _rg_prompt(gate.reshape(B, S, D), u.reshape(B, S, D), x.reshape(B, S, D), *rg_args)
    x_p, qt, (k_p, v_p, kb, vt) = layer0_post(x.reshape(Tp, D), p_p, tabs_p, (B, S))

    Ts = Sd * Bd
    tm = lambda a: jnp.swapaxes(a, 0, 1)
    tabs_s = _rope_tables(jnp.repeat(past_len + jnp.arange(Sd), Bd))
    p_s = jnp.swapaxes(p_sample, 1, 2).reshape(depth, Ts, -1)
    x, gate, u = layer0_pre(tm(x_sample).reshape(Ts, D))
    x, new_conv_s, new_h_s = _rg_sample(gate.reshape(Sd, Bd, D), u.reshape(Sd, Bd, D), x.reshape(Sd, Bd, D),
                                        tm(state_conv[0]), state_rglru[0], *rg_args)
    x_s, q, (k_s, v_s) = layer0_post(x.reshape(Ts, D), p_s, tabs_s, None)
    bm = lambda a: tm(a.reshape(Sd, Bd, -1))
    k_s, v_s = bm(k_s), bm(v_s)
    pad = lambda a: jnp.pad(a, ((0, 0), (0, -Sd % SUBLANES), (0, 0)))

    dn_p, dn_s = _attention(qt, kb.reshape(B, S, D), vt, bm(q), pad(k_s), pad(v_s), cache_k, cache_v, page_table,
                            *lam_args)
    y_prompt = layer1_post(x_p, dn_p.reshape(Tp, D), p_p).reshape(B, S, D)
    dn_s = dn_s.reshape(Bd, N_HEADS, Sd, V_DIM).transpose(2, 0, 1, 3).reshape(Ts, D)
    y_sample = bm(layer1_post(x_s, dn_s, p_s))

    return (y_prompt, y_sample,
            k_p.reshape(B, 2 * N_HEADS, HEAD_DIM, S).transpose(0, 3, 1, 2), v_p.reshape(B, S, N_HEADS, V_DIM),
            k_s.reshape(Bd, Sd, 2 * N_HEADS, HEAD_DIM), v_s.reshape(Bd, Sd, N_HEADS, V_DIM),
            new_conv_p[None], new_h_p.reshape(1, B, D),
            tm(new_conv_s)[None], new_h_s[None])
```

```python
import functools
import math

import jax
import jax.numpy as jnp
from jax import lax
from jax.experimental import pallas as pl
from jax.experimental.pallas import tpu as pltpu

F32 = jnp.float32
BF16 = jnp.bfloat16

EPS = 1e-6
N_HEADS = 8
HEAD_DIM = 64
V_DIM = 2 * HEAD_DIM
N_RG_BLOCKS = 8
CONV_W = 4
RG_C = 8.0
ROPE_THETA = 10000.0
LANES = 128
SUBLANES = 8
NEG_BIG = -1e30
LOG2_E = math.log2(math.e)
ONES_ROWS = 16
VMEM_LIMIT = 56 * 1024 * 1024

TOKEN_TILE = 512
SEQ_TILE = 512
ATTN_CHUNK = 512
ATTN_JOINT = 4
ATTN_HEADS_PER_STEP = 2


def _cparams(sem):
    return pltpu.CompilerParams(dimension_semantics=sem, vmem_limit_bytes=VMEM_LIMIT)


def _rms(x, g):
    return x * lax.rsqrt(jnp.mean(x * x, axis=-1, keepdims=True) + EPS) * g


def _rope(x, cos, sin_hi, sin_lo):
    outs = []
    for j in range(x.shape[1] // LANES):
        xj = x[:, j * LANES:(j + 1) * LANES]
        outs.append(xj * cos
                    + pltpu.roll(xj, LANES - HEAD_DIM // 2, 1) * sin_lo
                    + pltpu.roll(xj, HEAD_DIM // 2, 1) * sin_hi)
    return jnp.concatenate(outs, axis=1)


def _gelu_tanh(x):
    return 0.5 * x * (1.0 + jnp.tanh(math.sqrt(2.0 / math.pi) * (x + 0.044715 * (x * x * x))))


def _softplus(z):
    return jnp.maximum(z, 0.0) + jnp.log1p(jnp.exp(-jnp.abs(z)))


def _ffn_split(d_ff):
    for n_split in (2, 4, 1):
        if d_ff % (n_split * LANES) == 0:
            return n_split
    raise ValueError(f"unsupported FFN width {d_ff}")


def _store_head_transposed(dst_ref, x):
    n_chunks, n_rows, chunk = dst_ref.shape[2], dst_ref.shape[3], dst_ref.shape[4]
    for h in range(N_HEADS):
        xt = x[:, h * V_DIM:(h + 1) * V_DIM].T.astype(dst_ref.dtype)
        for c in range(n_chunks):
            dst_ref[0, h, c, 0:V_DIM, :] = xt[:, c * chunk:(c + 1) * chunk]
            if n_rows > V_DIM:
                dst_ref[0, h, c, V_DIM:n_rows, :] = jnp.ones((n_rows - V_DIM, chunk), dst_ref.dtype)


def _head_transposed_out(B, S, tm, dtype, extra_rows=0):
    tiles_per_seq = S // tm
    rows = V_DIM + extra_rows
    shape = jax.ShapeDtypeStruct((B, N_HEADS, S // ATTN_CHUNK, rows, ATTN_CHUNK), dtype)
    spec = pl.BlockSpec((1, N_HEADS, tm // ATTN_CHUNK, rows, ATTN_CHUNK),
                        lambda i: (i // tiles_per_seq, 0, i % tiles_per_seq, 0, 0))
    return shape, spec


def _table_specs(tm, n_tab_tiles):
    return [pl.BlockSpec((tm, LANES), lambda i: (i % n_tab_tiles, 0))] * 3


def _kv_store(x, g_ref, w_ref, cos_ref, shi_ref, slo_ref, k_ref, v_ref, *lowp_refs):
    n = _rms(x, g_ref[...]).astype(BF16)
    kv = jnp.dot(n, w_ref[...], preferred_element_type=F32)
    kd = N_HEADS * V_DIM
    k = _rope(kv[:, :kd], cos_ref[...], shi_ref[...], slo_ref[...])
    v = kv[:, kd:]
    if not lowp_refs:
        k_ref[...] = k
        v_ref[...] = v
    else:
        for h in range(N_HEADS):
            k_ref[0, h] = k[:, h * V_DIM:(h + 1) * V_DIM].T
        v_ref[...] = v
        kb_ref, vt_ref = lowp_refs
        kb_ref[...] = k.astype(kb_ref.dtype)
        _store_head_transposed(vt_ref, v)


def _kv_outputs(T, tm, kd, vd, seq_shape):
    out_shape = [jax.ShapeDtypeStruct((T, kd), F32), jax.ShapeDtypeStruct((T, vd), F32)]
    out_specs = [pl.BlockSpec((tm, kd), lambda i: (i, 0)), pl.BlockSpec((tm, vd), lambda i: (i, 0))]
    if seq_shape is not None:
        B, S = seq_shape
        tiles_per_seq = S // tm
        out_shape[0] = jax.ShapeDtypeStruct((B, N_HEADS, V_DIM, S), F32)
        out_specs[0] = pl.BlockSpec((1, N_HEADS, V_DIM, tm), lambda i: (i // tiles_per_seq, 0, 0, i % tiles_per_seq))
        vt_shape, vt_spec = _head_transposed_out(B, S, tm, BF16, extra_rows=ONES_ROWS)
        out_shape += [jax.ShapeDtypeStruct((T, kd), BF16), vt_shape]
        out_specs += [pl.BlockSpec((tm, kd), lambda i: (i, 0)), vt_spec]
    return out_shape, out_specs


def _block_body(*refs, pre, post, n_split, transposed_q, final, with_kv):
    refs = list(refs)
    take = lambda k: [refs.pop(0) for _ in range(k)]
    (x_ref,) = take(1)
    x = x_ref[...]
    if pre == "attn_out":
        dn_ref, wo_ref = take(2)
        x = x + jnp.dot(dn_ref[...].astype(BF16), wo_ref[...], preferred_element_type=F32)
    g_ref, wgu_ref, wd_ref = take(3)

    n = _rms(x, g_ref[...]).astype(BF16)
    d_ff = wd_ref.shape[0]
    tf = d_ff // n_split
    y = None
    for c in range(n_split):
        g = jnp.dot(n, wgu_ref[:, c * tf:(c + 1) * tf], preferred_element_type=F32)
        u = jnp.dot(n, wgu_ref[:, d_ff + c * tf:d_ff + (c + 1) * tf], preferred_element_type=F32)
        h = (g * jax.nn.sigmoid(g) * u).astype(BF16)
        part = jnp.dot(h, wd_ref[c * tf:(c + 1) * tf, :], preferred_element_type=F32)
        y = part if y is None else y + part
    x = x + 0.5 * y

    if post == "rg_in":
        mg_ref, win_ref, x_out, gate_ref, u_ref = take(5)
        x_out[...] = x
        z = jnp.dot(_rms(x, mg_ref[...]).astype(BF16), win_ref[...], preferred_element_type=F32)
        d = gate_ref.shape[1]
        gate_ref[...] = z[:, :d]
        u_ref[...] = z[:, d:]
    elif post == "q":
        mg_ref, wq_ref, cos_ref, shi_ref, slo_ref, x_out, q_ref = take(7)
        x_out[...] = x
        q = jnp.dot(_rms(x, mg_ref[...]).astype(BF16), wq_ref[...], preferred_element_type=F32)
        q = _rope(q, cos_ref[...], shi_ref[...], slo_ref[...]) * (HEAD_DIM ** -0.5)
        if transposed_q:
            _store_head_transposed(q_ref, q * LOG2_E)
        else:
            q_ref[...] = q
    else:
        p_ref, pg_ref, wpg_ref, wpp_ref, fin_ref = take(5)
        kv_in = take(5) if with_kv else None
        (x_out,) = take(1)
        gate = jax.nn.sigmoid(jnp.dot(_rms(x, pg_ref[...]).astype(BF16), wpg_ref[...], preferred_element_type=F32))
        x = x + gate * jnp.dot(p_ref[...].astype(BF16), wpp_ref[...], preferred_element_type=F32)
        x_out[...] = _rms(x, fin_ref[...]) if final else x
        if with_kv:
            _kv_store(x, *kv_in, *refs)


def _ffn_block(x, g, w_gu_all, w_down_all, layer, post, *, attn_out=None, final=False):
    T, D = x.shape
    d_ff = w_down_all.shape[1]
    tm = min(TOKEN_TILE, T)
    resident = pl.Buffered(1)
    tok = lambda width: pl.BlockSpec((tm, width), lambda i: (i, 0))
    vec = pl.BlockSpec((1, D), lambda i: (0, 0))
    whole = lambda a: pl.BlockSpec(a.shape, lambda i: (0,) * a.ndim, pipeline_mode=resident)
    layer_of = lambda a: pl.BlockSpec((None,) + a.shape[1:], lambda i: (layer, 0, 0), pipeline_mode=resident)
    row = lambda a: a.reshape(1, D)

    args, in_specs = [x], [tok(D)]
    if attn_out is not None:
        dn, w_o = attn_out
        args += [dn, w_o]
        in_specs += [tok(dn.shape[1]), whole(w_o)]
    args += [row(g), w_gu_all, w_down_all]
    in_specs += [vec, layer_of(w_gu_all), layer_of(w_down_all)]

    kind = post[0]
    transposed_q = False
    if kind == "rg_in":
        _, mix_g, w_in = post
        N = w_in.shape[1] // 2
        args += [row(mix_g), w_in]
        in_specs += [vec, whole(w_in)]
        out_shape = (jax.ShapeDtypeStruct((T, D), F32),) + (jax.ShapeDtypeStruct((T, N), F32),) * 2
        out_specs = (tok(D), tok(N), tok(N))
    elif kind == "q":
        _, mix_g, w_q, tabs, seq_shape = post
        args += [row(mix_g), w_q, *tabs]
        in_specs += [vec, whole(w_q)] + _table_specs(tm, tabs[0].shape[0] // tm)
        transposed_q = seq_shape is not None
        if transposed_q:
            q_shape, q_spec = _head_transposed_out(*seq_shape, tm, BF16)
        else:
            q_shape, q_spec = jax.ShapeDtypeStruct((T, w_q.shape[1]), F32), tok(w_q.shape[1])
        out_shape = (jax.ShapeDtypeStruct((T, D), F32), q_shape)
        out_specs = (tok(D), q_spec)
    else:
        _, p_all, ple_g, w_gate_all, w_proj_all, fin_g, kv = post
        P = p_all.shape[2]
        args += [p_all, row(ple_g), w_gate_all, w_proj_all, row(fin_g)]
        in_specs += [pl.BlockSpec((None, tm, P), lambda i: (layer, i, 0)), vec,
                     layer_of(w_gate_all), layer_of(w_proj_all), vec]
        out_shape = jax.ShapeDtypeStruct((T, D), F32)
        out_specs = tok(D)
        if kv is not None:
            kv_g, w_kv, tabs, kd, seq_shape = kv
            args += [row(kv_g), w_kv, *tabs]
            in_specs += [vec, whole(w_kv)] + _table_specs(tm, tabs[0].shape[0] // tm)
            kv_shapes, kv_specs = _kv_outputs(T, tm, kd, w_kv.shape[1] - kd, seq_shape)
            out_shape = (out_shape, *kv_shapes)
            out_specs = (out_specs, *kv_specs)
    return pl.pallas_call(
        functools.partial(_block_body, pre=None if attn_out is None else "attn_out", post=kind,
                          n_split=_ffn_split(d_ff), transposed_q=transposed_q, final=final,
                          with_kv=kind == "ple" and post[-1] is not None),
        out_shape=out_shape,
        grid=(T // tm,),
        in_specs=in_specs,
        out_specs=out_specs,
        compiler_params=_cparams(("parallel",)),
        name="ffn_" + kind,
    )(*args)


def _rg_coeffs(conv, wai_ref, ba, bi, sp):
    cb = conv.astype(BF16)
    nb = wai_ref.shape[0]
    blk = conv.shape[1] // nb
    a_parts, b_parts = [], []
    for n in range(nb):
        sl = slice(n * blk, (n + 1) * blk)
        g = jnp.dot(cb[:, sl], wai_ref[n], preferred_element_type=F32)
        r = jax.nn.sigmoid(g[:, :blk] + ba[:, sl])
        i = jax.nn.sigmoid(g[:, blk:] + bi[:, sl])
        log_a = -RG_C * r * sp[:, sl]
        a = jnp.exp(log_a)
        mult = jnp.sqrt(-jnp.tanh(log_a) * (a * a + 1.0))
        a_parts.append(a)
        b_parts.append(mult * (i * conv[:, sl]))
    return a_parts, b_parts


def _rg_prompt_body(gate_ref, u_ref, x_ref, cw_ref, cb_ref, wai_ref, ba_ref, bi_ref, lam_ref, wo_ref,
                    o_ref, nconv_ref, nh_ref, ubuf_ref, a_ref, h_ref, hstate_ref):
    s = pl.program_id(1)
    ts, D = u_ref.shape[1], u_ref.shape[2]

    @pl.when(s == 0)
    def _():
        ubuf_ref[0:SUBLANES, :] = jnp.zeros((SUBLANES, D), F32)
        hstate_ref[...] = jnp.zeros_like(hstate_ref)

    u = u_ref[0]
    ubuf_ref[SUBLANES:SUBLANES + ts, :] = u
    cw = cw_ref[...]
    conv = cb_ref[...] + u * cw[CONV_W - 1:CONV_W, :]
    for k in range(1, CONV_W):
        conv = conv + ubuf_ref[SUBLANES - k:SUBLANES - k + ts, :] * cw[CONV_W - 1 - k:CONV_W - k, :]

    sp = _softplus(-lam_ref[...])
    a, b = _rg_coeffs(conv, wai_ref, ba_ref[...], bi_ref[...], sp)
    a_ref[...] = jnp.concatenate(a, axis=1)
    h_ref[...] = jnp.concatenate(b, axis=1)

    row = lax.broadcasted_iota(jnp.int32, (SUBLANES, D), 0)

    def group(gi, h_prev):
        r0 = pl.multiple_of(gi * SUBLANES, SUBLANES)
        ag = a_ref[pl.ds(r0, SUBLANES), :]
        bg = h_ref[pl.ds(r0, SUBLANES), :]
        for d in (1, 2, 4):
            m = row >= d
            a_sh = pltpu.roll(ag, d, 0)
            b_sh = pltpu.roll(bg, d, 0)
            bg = jnp.where(m, ag * b_sh + bg, bg)
            ag = jnp.where(m, ag * a_sh, ag)
        hg = ag * h_prev + bg
        h_ref[pl.ds(r0, SUBLANES), :] = hg
        return jnp.broadcast_to(hg[SUBLANES - 1:SUBLANES, :], (SUBLANES, D))

    h_last = lax.fori_loop(0, ts // SUBLANES, group, hstate_ref[...])
    hstate_ref[...] = h_last
    ubuf_ref[0:SUBLANES, :] = u[ts - SUBLANES:ts, :]

    y = (_gelu_tanh(gate_ref[0]) * h_ref[...]).astype(BF16)
    o_ref[0] = x_ref[0] + jnp.dot(y, wo_ref[...], preferred_element_type=F32)

    @pl.when(s == pl.num_programs(1) - 1)
    def _():
        nconv_ref[0] = u[ts - (CONV_W - 1):ts, :]
        nh_ref[0] = h_last[0:1, :]


def _rg_prompt(gate, u, x, cw, cb, wai, ba, bi, lam, wo):
    B, S, D = u.shape
    ts = min(SEQ_TILE, S)
    row = lambda a: a.reshape(1, D)
    tok = pl.BlockSpec((1, ts, D), lambda b, s: (b, s, 0))
    vec = pl.BlockSpec((1, D), lambda b, s: (0, 0))
    return pl.pallas_call(
        _rg_prompt_body,
        out_shape=(jax.ShapeDtypeStruct((B, S, D), F32),
                   jax.ShapeDtypeStruct((B, CONV_W - 1, D), F32),
                   jax.ShapeDtypeStruct((B, 1, D), F32)),
        grid=(B, S // ts),
        in_specs=[tok, tok, tok,
                  pl.BlockSpec((CONV_W, D), lambda b, s: (0, 0)), vec,
                  pl.BlockSpec(wai.shape, lambda b, s: (0, 0, 0)), vec, vec, vec,
                  pl.BlockSpec((D, D), lambda b, s: (0, 0))],
        out_specs=(tok,
                   pl.BlockSpec((1, CONV_W - 1, D), lambda b, s: (b, 0, 0)),
                   pl.BlockSpec((1, 1, D), lambda b, s: (b, 0, 0))),
        scratch_shapes=[pltpu.VMEM((ts + SUBLANES, D), F32), pltpu.VMEM((ts, D), F32),
                        pltpu.VMEM((ts, D), F32), pltpu.VMEM((SUBLANES, D), F32)],
        compiler_params=_cparams(("parallel", "arbitrary")),
        name="rg_prompt",
    )(gate, u, x, cw, row(cb), wai, row(ba), row(bi), row(lam), wo)


def _rg_sample_body(gate_ref, u_ref, x_ref, cprev_ref, h0_ref, cw_ref, cb_ref, wai_ref, ba_ref, bi_ref,
                    lam_ref, wo_ref, o_ref, nconv_ref, nh_ref):
    n_t = u_ref.shape[0]
    cw = cw_ref[...]
    hist = [cprev_ref[k] for k in range(CONV_W - 1)] + [u_ref[t] for t in range(n_t)]
    sp = _softplus(-lam_ref[...])
    h = h0_ref[...]
    for t in range(n_t):
        conv = cb_ref[...] + hist[t] * cw[0:1, :]
        for k in range(1, CONV_W):
            conv = conv + hist[t + k] * cw[k:k + 1, :]
        a, b = _rg_coeffs(conv, wai_ref, ba_ref[...], bi_ref[...], sp)
        h = jnp.concatenate(a, axis=1) * h + jnp.concatenate(b, axis=1)
        y = (_gelu_tanh(gate_ref[t]) * h).astype(BF16)
        o_ref[t] = x_ref[t] + jnp.dot(y, wo_ref[...], preferred_element_type=F32)
    for k in range(CONV_W - 1):
        nconv_ref[k] = hist[n_t + k]
    nh_ref[...] = h


def _rg_sample(gate, u, x, cprev, h0, cw, cb, wai, ba, bi, lam, wo):
    n_t, Bd, D = u.shape
    row = lambda a: a.reshape(1, D)
    return pl.pallas_call(
        _rg_sample_body,
        out_shape=(jax.ShapeDtypeStruct((n_t, Bd, D), F32),
                   jax.ShapeDtypeStruct((CONV_W - 1, Bd, D), F32),
                   jax.ShapeDtypeStruct((Bd, D), F32)),
        compiler_params=pltpu.CompilerParams(vmem_limit_bytes=VMEM_LIMIT),
        name="rg_sample",
    )(gate, u, x, cprev, h0, cw, row(cb), wai, row(ba), row(bi), row(lam), wo)


def _diff_lambda(lq1, lk1, lq2, lk2, lambda_init):
    return (jnp.exp(jnp.sum(lq1 * lk1, axis=-1, keepdims=True))
            - jnp.exp(jnp.sum(lq2 * lk2, axis=-1, keepdims=True)) + lambda_init)


def _diff_norm(o0, o1, lam, subln, lambda_init):
    d = o0 - lam * o1
    return _rms(d, subln) * (1.0 - lambda_init)


def _flash_body(qt_ref, k_ref, vt_ref, lq1_ref, lk1_ref, lq2_ref, lk2_ref, sub_ref, o_ref, w_ref, acc_ref,
                m_ref, *, lambda_init):
    i = pl.program_id(2)
    n_heads = qt_ref.shape[1]
    tq = qt_ref.shape[4]
    chunk = vt_ref.shape[4]

    for hh in range(n_heads):
        qt = qt_ref[0, hh, 0]
        feat = lax.broadcasted_iota(jnp.int32, qt.shape, 0)
        zero = jnp.zeros_like(qt)
        w_ref[hh, :, 0:tq] = jnp.where(feat < HEAD_DIM, qt, zero)
        w_ref[hh, :, tq:2 * tq] = jnp.where(feat < HEAD_DIM, zero, qt)
    acc_ref[...] = jnp.zeros_like(acc_ref)
    m_ref[...] = jnp.full_like(m_ref, NEG_BIG)

    def attend(hh, first, n_full, diagonal):
        cs = [first + t for t in range(n_full)] + ([i] if diagonal else [])
        s = []
        for t, c in enumerate(cs):
            r0 = pl.multiple_of(c * chunk, chunk)
            st = jnp.dot(k_ref[0, pl.ds(r0, chunk), V_DIM * hh:V_DIM * (hh + 1)], w_ref[hh],
                         preferred_element_type=F32)
            if diagonal and t == n_full:
                key = lax.broadcasted_iota(jnp.int32, st.shape, 0)
                qry = lax.broadcasted_iota(jnp.int32, st.shape, 1)
                qry = jnp.where(qry >= tq, qry - tq, qry)
                st = jnp.where(key <= qry, st, NEG_BIG)
            s.append(st)
        m_prev = m_ref[hh]
        m_new = m_prev
        for st in s:
            m_new = jnp.maximum(m_new, jnp.max(st, axis=0, keepdims=True))
        alpha = jnp.exp2(m_prev - m_new)
        pv = None
        for st, c in zip(s, cs):
            p = jnp.exp2(st - m_new).astype(BF16)
            part = jnp.dot(vt_ref[0, hh, c], p, preferred_element_type=F32)
            pv = part if pv is None else pv + part
        acc_ref[hh] = alpha * acc_ref[hh] + pv
        m_ref[hh] = m_new

    def main(g, carry):
        for hh in range(n_heads):
            attend(hh, g * ATTN_JOINT, ATTN_JOINT, False)
        return carry

    lax.fori_loop(0, i // ATTN_JOINT, main, 0)
    for r in range(ATTN_JOINT):
        @pl.when(i % ATTN_JOINT == r)
        def _():
            for hh in range(n_heads):
                attend(hh, i - r, r, True)

    lam = _diff_lambda(lq1_ref[...], lk1_ref[...], lq2_ref[...], lk2_ref[...], lambda_init)
    for hh in range(n_heads):
        acc = acc_ref[hh]
        o = acc[0:V_DIM] / acc[V_DIM:V_DIM + 1]
        d = (o[:, 0:tq] - lam * o[:, tq:2 * tq]).T
        o_ref[0, :, V_DIM * hh:V_DIM * (hh + 1)] = (_rms(d, sub_ref[...]) * (1.0 - lambda_init)).astype(o_ref.dtype)


def _paged_body(q_ref, kn_ref, vn_ref, kt_refs, v_refs, lam_refs, o_ref, qblk_ref, m_ref, l_ref, acc_ref,
                lambda_init):
    lq1_ref, lk1_ref, lq2_ref, lk2_ref, sub_ref = lam_refs
    n_q = q_ref.shape[1]
    page = kt_refs[0].shape[2]
    grp = 2 * n_q

    def update(s, head_pv):
        m_prev = m_ref[...]
        m_new = jnp.maximum(m_prev, jnp.max(s, axis=1, keepdims=True))
        alpha = jnp.exp(m_prev - m_new)
        p = jnp.exp(s - m_new)
        l_ref[...] = alpha * l_ref[...] + jnp.sum(p, axis=1, keepdims=True)
        m_ref[...] = m_new
        for h in range(N_HEADS):
            sl = slice(grp * h, grp * (h + 1))
            acc_ref[sl, :] = alpha[sl] * acc_ref[sl, :] + head_pv(p[sl], h)

    q = q_ref[0]
    r = lax.broadcasted_iota(jnp.int32, qblk_ref.shape, 0)
    c = lax.broadcasted_iota(jnp.int32, qblk_ref.shape, 1)
    qrep = jnp.zeros(qblk_ref.shape, F32)
    for t in range(n_q):
        qrep = jnp.where(r % n_q == t, jnp.broadcast_to(q[t:t + 1, :], qblk_ref.shape), qrep)
    qblk_ref[...] = jnp.where(c // HEAD_DIM == r // n_q, qrep, 0.0)
    m_ref[...] = jnp.full_like(m_ref, NEG_BIG)
    l_ref[...] = jnp.zeros_like(l_ref)
    acc_ref[...] = jnp.zeros_like(acc_ref)
    kn, vn = kn_ref[0], vn_ref[0]
    s = lax.dot_general(qblk_ref[...], kn, (((1,), (1,)), ((), ())), preferred_element_type=F32)
    rn = lax.broadcasted_iota(jnp.int32, s.shape, 0)
    cn = lax.broadcasted_iota(jnp.int32, s.shape, 1)
    s = jnp.where(cn <= rn % n_q, s, NEG_BIG)
    update(s, lambda p, h: jnp.dot(p, vn[:, V_DIM * h:V_DIM * (h + 1)], preferred_element_type=F32))

    qblk = qblk_ref[...]
    s = jnp.concatenate([jnp.dot(qblk, kt_ref[0], preferred_element_type=F32) for kt_ref in kt_refs], axis=1)

    def head_pv(p, h):
        out = None
        for g, v_ref in enumerate(v_refs):
            part = jnp.dot(p[:, page * g:page * (g + 1)], v_ref[0, pl.ds(h, page, stride=N_HEADS), :],
                           preferred_element_type=F32)
            out = part if out is None else out + part
        return out

    update(s, head_pv)

    o = acc_ref[...] / l_ref[...]
    lam = _diff_lambda(lq1_ref[...], lk1_ref[...], lq2_ref[...], lk2_ref[...], lambda_init)
    for h in range(N_HEADS):
        blk = o[grp * h:grp * (h + 1)]
        o_ref[0, n_q * h:n_q * (h + 1), :] = _diff_norm(blk[0:n_q], blk[n_q:grp], lam, sub_ref[...], lambda_init)


def _attn_body(pt_ref, qt_ref, k_ref, vt_ref, q_ref, kn_ref, vn_ref, *rest, n_pages, lambda_init):
    kt_refs, v_refs = rest[:n_pages], rest[n_pages:2 * n_pages]
    lam_refs = rest[2 * n_pages:2 * n_pages + 5]
    o_prompt_ref, o_sample_ref, w_ref, acc_ref, m_ref, qblk_ref, pm_ref, pl_ref, pacc_ref = rest[2 * n_pages + 5:]
    _flash_body(qt_ref, k_ref, vt_ref, *lam_refs, o_prompt_ref, w_ref, acc_ref, m_ref, lambda_init=lambda_init)
    _paged_body(q_ref, kn_ref, vn_ref, kt_refs, v_refs, lam_refs, o_sample_ref, qblk_ref, pm_ref, pl_ref, pacc_ref,
                lambda_init)


def _attention(qt, kb, vt, q, k_new, v_new, cache_k, cache_v, page_table, lq1, lk1, lq2, lk2, subln, lambda_init):
    B, S, W = kb.shape
    n_chunks, chunk = qt.shape[2], qt.shape[4]
    hp = ATTN_HEADS_PER_STEP
    n_groups = N_HEADS // hp
    Bd, n_q, _ = q.shape
    n_pool, page = cache_k.shape[0], cache_k.shape[1]
    n_pages = page_table.shape[1]
    assert B * n_groups * n_chunks == Bd, "one sample sequence per prompt-attention grid step"
    rows = 2 * N_HEADS * n_q
    ckt = cache_k.transpose(0, 2, 3, 1).reshape(n_pool, W, page)
    cv = cache_v.reshape(n_pool, page * N_HEADS, V_DIM)
    pt = page_table.reshape(-1)

    seq = lambda b, h, i: (b * n_groups + h) * n_chunks + i
    held = pl.Buffered(1)
    vec = lambda n: pl.BlockSpec((1, n), lambda b, h, i, pt: (0, 0))
    per_seq = lambda a: pl.BlockSpec((1,) + a.shape[1:], lambda b, h, i, pt: (seq(b, h, i), 0, 0))

    def page_spec(shape, g):
        return pl.BlockSpec((1,) + shape[1:], lambda b, h, i, pt: (pt[seq(b, h, i) * n_pages + g], 0, 0))

    row = lambda a: a.reshape(1, -1)
    return pl.pallas_call(
        functools.partial(_attn_body, n_pages=n_pages, lambda_init=lambda_init),
        out_shape=(jax.ShapeDtypeStruct((B, S, W), BF16), jax.ShapeDtypeStruct((Bd, N_HEADS * n_q, V_DIM), F32)),
        grid_spec=pltpu.PrefetchScalarGridSpec(
            num_scalar_prefetch=1,
            grid=(B, n_groups, n_chunks),
            in_specs=[
                pl.BlockSpec((1, hp, 1, V_DIM, chunk), lambda b, h, i, pt: (b, h, i, 0, 0)),
                pl.BlockSpec((1, S, hp * V_DIM), lambda b, h, i, pt: (b, 0, h), pipeline_mode=held),
                pl.BlockSpec((1, hp, n_chunks, vt.shape[3], chunk), lambda b, h, i, pt: (b, h, 0, 0, 0),
                             pipeline_mode=held),
                per_seq(q), per_seq(k_new), per_seq(v_new)]
                + [page_spec(ckt.shape, g) for g in range(n_pages)]
                + [page_spec(cv.shape, g) for g in range(n_pages)]
                + [vec(HEAD_DIM)] * 4 + [vec(V_DIM)],
            out_specs=(pl.BlockSpec((1, chunk, hp * V_DIM), lambda b, h, i, pt: (b, i, h)),
                       pl.BlockSpec((1, N_HEADS * n_q, V_DIM), lambda b, h, i, pt: (seq(b, h, i), 0, 0))),
            scratch_shapes=[pltpu.VMEM((hp, V_DIM, 2 * chunk), BF16), pltpu.VMEM((hp, vt.shape[3], 2 * chunk), F32),
                            pltpu.VMEM((hp, 1, 2 * chunk), F32),
                            pltpu.VMEM((rows, W), F32), pltpu.VMEM((rows, 1), F32),
                            pltpu.VMEM((rows, 1), F32), pltpu.VMEM((rows, V_DIM), F32)],
        ),
        compiler_params=_cparams(("arbitrary", "arbitrary", "arbitrary")),
        name="diff_attention",
    )(pt, qt, kb, vt, q, k_new, v_new, *([ckt] * n_pages), *([cv] * n_pages),
      row(lq1), row(lk1), row(lq2), row(lk2), row(subln))


def _rope_tables(pos):
    half = HEAD_DIM // 2
    lane = jnp.arange(LANES)
    inv = jnp.power(ROPE_THETA, -(lane % half).astype(F32) * 2.0 / HEAD_DIM)
    ang = pos.astype(F32)[:, None] * inv[None, :]
    cos, sin = jnp.cos(ang), jnp.sin(ang)
    upper = (lane % HEAD_DIM) >= half
    return cos, jnp.where(upper, sin, 0.0), jnp.where(upper, 0.0, -sin)


def kernel(x_prompt, x_sample, p_prompt, p_sample, cache_k, cache_v, page_table, state_conv, state_rglru, ffn1_norm, ffn1_w_gu, ffn1_w_down, mix_norm, rg_w_in, rg_conv_w, rg_conv_b, rg_w_a, rg_b_a, rg_w_i, rg_b_i, rg_lambda, rg_w_out, kv_norm, w_kv, attn_w_q, lambda_q1, lambda_k1, lambda_q2, lambda_k2, attn_subln, attn_w_o, ffn2_norm, ffn2_w_gu, ffn2_w_down, ple_norm, ple_w_gate, ple_w_proj, final_norm):
    B, S, D = x_prompt.shape
    Bd, Sd, _ = x_sample.shape
    depth = ffn1_norm.shape[0]
    n_a = rg_w_in.shape[0]
    assert depth == 2 and n_a == 1, "one recurrent layer followed by one attention layer"
    assert D == N_HEADS * V_DIM
    n_pages, page = page_table.shape[1], cache_k.shape[1]
    past_len = n_pages * page
    kd = 2 * N_HEADS * HEAD_DIM
    lambda_init = 0.8 - 0.6 * math.exp(-0.3 * 1)

    bf = lambda w: w.astype(BF16)
    w_gu1, w_dn1, w_gu2, w_dn2 = bf(ffn1_w_gu), bf(ffn1_w_down), bf(ffn2_w_gu), bf(ffn2_w_down)
    w_in, w_out, w_kvb, w_q, w_o = bf(rg_w_in[0]), bf(rg_w_out[0]), bf(w_kv), bf(attn_w_q[0]), bf(attn_w_o[0])
    w_pg, w_pp = bf(ple_w_gate), bf(ple_w_proj)
    wai = bf(jnp.concatenate([rg_w_a[0], rg_w_i[0]], axis=-1))

    def layer0_pre(x):
        return _ffn_block(x, ffn1_norm[0], w_gu1, w_dn1, 0, ("rg_in", mix_norm[0], w_in))

    def layer0_post(x, p_all, tabs, seq_shape):
        x, *kv = _ffn_block(x, ffn2_norm[0], w_gu2, w_dn2, 0,
                            ("ple", p_all, ple_norm[0], w_pg, w_pp, final_norm, (kv_norm, w_kvb, tabs, kd, seq_shape)))
        x, q = _ffn_block(x, ffn1_norm[1], w_gu1, w_dn1, 1, ("q", mix_norm[1], w_q, tabs, seq_shape))
        return x, q, kv

    def layer1_post(x, dn, p_all):
        return _ffn_block(x, ffn2_norm[1], w_gu2, w_dn2, 1, ("ple", p_all, ple_norm[1], w_pg, w_pp, final_norm, None),
                          attn_out=(dn, w_o), final=True)

    rg_args = (rg_conv_w[0], rg_conv_b[0], wai, rg_b_a[0], rg_b_i[0], rg_lambda[0], w_out)
    lam_args = (lambda_q1[0], lambda_k1[0], lambda_q2[0], lambda_k2[0], attn_subln[0], lambda_init)

    Tp = B * S
    tabs_p = _rope_tables(jnp.arange(S))
    p_p = p_prompt.reshape(depth, Tp, -1)
    x, gate, u = layer0_pre(x_prompt.reshape(Tp, D))
    x, new_conv_p, new_h_p = _rg_prompt(gate.reshape(B, S, D), u.reshape(B, S, D), x.reshape(B, S, D), *rg_args)
    x_p, qt, (k_p, v_p, kb, vt) = layer0_post(x.reshape(Tp, D), p_p, tabs_p, (B, S))

    Ts = Sd * Bd
    tm = lambda a: jnp.swapaxes(a, 0, 1)
    tabs_s = _rope_tables(jnp.repeat(past_len + jnp.arange(Sd), Bd))
    p_s = jnp.swapaxes(p_sample, 1, 2).reshape(depth, Ts, -1)
    x, gate, u = layer0_pre(tm(x_sample).reshape(Ts, D))
    x, new_conv_s, new_h_s = _rg_sample(gate.reshape(Sd, Bd, D), u.reshape(Sd, Bd, D), x.reshape(Sd, Bd, D),
                                        tm(state_conv[0]), state_rglru[0], *rg_args)
    x_s, q, (k_s, v_s) = layer0_post(x.reshape(Ts, D), p_s, tabs_s, None)
    bm = lambda a: tm(a.reshape(Sd, Bd, -1))
    k_s, v_s = bm(k_s), bm(v_s)
    pad = lambda a: jnp.pad(a, ((0, 0), (0, -Sd % SUBLANES), (0, 0)))

    dn_p, dn_s = _attention(qt, kb.reshape(B, S, D), vt, bm(q), pad(k_s), pad(v_s), cache_k, cache_v, page_table,
                            *lam_args)
    y_prompt = layer1_post(x_p, dn_p.reshape(Tp, D), p_p).reshape(B, S, D)
    dn_s = dn_s.reshape(Bd, N_HEADS, Sd, V_DIM).transpose(2, 0, 1, 3).reshape(Ts, D)
    y_sample = bm(layer1_post(x_s, dn_s, p_s))

    return (y_prompt, y_sample,
            k_p.reshape(B, 2 * N_HEADS, HEAD_DIM, S).transpose(0, 3, 1, 2), v_p.reshape(B, S, N_HEADS, V_DIM),
            k_s.reshape(Bd, Sd, 2 * N_HEADS, HEAD_DIM), v_s.reshape(Bd, Sd, N_HEADS, V_DIM),
            new_conv_p[None], new_h_p.reshape(1, B, D),
            tm(new_conv_s)[None], new_h_s[None])
```
